```python
import jax, jax.numpy as jnp
from jax import lax
import numpy as np

D_MODEL = 1024
BATCH = 32
SEQ = 2048
DEPTH = 1

N_META = 16
GDN_HEADS = 8
GDN_DK = 128
GDN_DV = 128
CONV_WIDTH = 4
CHUNK = 64
SWA_Q_HEADS = 16
SWA_KV_HEADS = 4
SWA_HEAD_DIM = 64
SWA_GROUP = SWA_Q_HEADS // SWA_KV_HEADS
WINDOW = 128
ATTN_BLOCK = 128
N_EXPERTS = 32
TOP_K = 4
D_FF = 1024
SWIGLU_ALPHA = 1.702
SWIGLU_LIMIT = 7.0
MOE_BLOCK = 512
LN_EPS = 1e-5
RMS_EPS = 1e-6
DEEPNORM_ALPHA = (2.0 * DEPTH) ** 0.25
DEEPNORM_BETA = (8.0 * DEPTH) ** -0.25

GDN_QK = GDN_HEADS * GDN_DK
GDN_VD = GDN_HEADS * GDN_DV
SWA_QD = SWA_Q_HEADS * SWA_HEAD_DIM
SWA_KVD = SWA_KV_HEADS * SWA_HEAD_DIM
SPLIT_SIZES = (GDN_QK, GDN_QK, GDN_VD, GDN_VD, GDN_HEADS, GDN_HEADS, SWA_QD, SWA_KVD, SWA_KVD, D_MODEL, D_MODEL)
C_IN = sum(SPLIT_SIZES)
SPLIT_POINTS = tuple(int(p) for p in np.cumsum(SPLIT_SIZES)[:-1])

kernel_name = 'hybrid_gdn_swa_sink_moe_block'


def layer_norm(x, g, b):
    xf = x.astype(jnp.float32)
    mu = jnp.mean(xf, -1, keepdims=True)
    var = jnp.mean(jnp.square(xf - mu), -1, keepdims=True)
    y = (xf - mu) * lax.rsqrt(var + LN_EPS) * g.astype(jnp.float32) + b.astype(jnp.float32)
    return y.astype(x.dtype)


def l2_normalize(t):
    tf = t.astype(jnp.float32)
    return tf * lax.rsqrt(jnp.sum(tf * tf, -1, keepdims=True) + RMS_EPS)


def causal_dwconv_silu(x, w):
    L = x.shape[1]
    xp = jnp.pad(x, ((0, 0), (CONV_WIDTH - 1, 0), (0, 0)))
    y = xp[:, 0:L] * w[:, 0]
    for i in range(1, CONV_WIDTH):
        y = y + xp[:, i:i + L] * w[:, i]
    return jax.nn.silu(y)


def gated_deltanet(q, k, v, z, b, a, conv_w, a_log, dt_bias, norm_w):
    f32 = jnp.float32
    bsz, L, _ = q.shape
    qkv = causal_dwconv_silu(jnp.concatenate([q, k, v], -1), conv_w)
    q, k, v = jnp.split(qkv, [GDN_QK, 2 * GDN_QK], axis=-1)
    q = l2_normalize(q.reshape(bsz, L, GDN_HEADS, GDN_DK)) * GDN_DK ** -0.5
    k = l2_normalize(k.reshape(bsz, L, GDN_HEADS, GDN_DK))
    v = v.reshape(bsz, L, GDN_HEADS, GDN_DV).astype(f32)
    beta = jax.nn.sigmoid(b.astype(f32))
    log_g = -jnp.exp(a_log.astype(f32)) * jax.nn.softplus(a.astype(f32) + dt_bias.astype(f32))
    n_pad = (-N_META) % CHUNK

    def to_chunks(t):
        t = jnp.pad(t, ((0, 0), (n_pad, 0)) + ((0, 0),) * (t.ndim - 2))
        t = t.reshape((bsz, -1, CHUNK) + t.shape[2:])
        return jnp.moveaxis(t, 3, 1)

    qc, kc, vc, bc, gc = [to_chunks(t) for t in (q, k, v, beta, log_g)]
    gam = jnp.cumsum(gc, axis=-1)
    pos = jnp.arange(CHUNK)
    strict = pos[:, None] > pos[None, :]
    incl = pos[:, None] >= pos[None, :]
    decay = jnp.exp(jnp.where(incl, gam[..., :, None] - gam[..., None, :], -jnp.inf))
    kk = jnp.einsum('bhnck,bhnsk->bhncs', kc, kc)
    a_mat = jnp.where(strict, kk * decay, 0.0) * bc[..., :, None]
    eye = jnp.eye(CHUNK, dtype=f32)
    rhs = jnp.concatenate([vc * bc[..., None], kc * (bc * jnp.exp(gam))[..., None]], -1)
    sol = lax.linalg.triangular_solve(a_mat + eye, rhs, left_side=True, lower=True, unit_diagonal=True)
    u_v, w = sol[..., :GDN_DV], sol[..., GDN_DV:]
    qk = jnp.einsum('bhnck,bhnsk->bhncs', qc, kc) * decay
    q_dec = qc * jnp.exp(gam)[..., None]
    k_dec = kc * jnp.exp(gam[..., -1:] - gam)[..., None]
    g_tot = jnp.exp(gam[..., -1])

    def chunk_step(state, inp):
        u_v_c, w_c, qk_c, q_c, k_c, g_c = inp
        u = u_v_c - jnp.einsum('bhck,bhkv->bhcv', w_c, state)
        o = jnp.einsum('bhck,bhkv->bhcv', q_c, state) + jnp.einsum('bhcs,bhsv->bhcv', qk_c, u)
        state = state * g_c[..., None, None] + jnp.einsum('bhck,bhcv->bhkv', k_c, u)
        return state, o

    xs = tuple(jnp.moveaxis(t, 2, 0) for t in (u_v, w, qk, q_dec, k_dec, g_tot))
    state0 = jnp.zeros((bsz, GDN_HEADS, GDN_DK, GDN_DV), f32)
    _, o = lax.scan(chunk_step, state0, xs)
    o = jnp.moveaxis(o, 0, 2).reshape(bsz, GDN_HEADS, -1, GDN_DV).transpose(0, 2, 1, 3)[:, n_pad:]
    o = o * lax.rsqrt(jnp.mean(o * o, -1, keepdims=True) + RMS_EPS) * norm_w.astype(f32)
    o = o * jax.nn.silu(z.reshape(bsz, L, GDN_HEADS, GDN_DV).astype(f32))
    return o.reshape(bsz, L, GDN_VD).astype(z.dtype)


def sliding_window_attention(q, k, v, sinks):
    f32 = jnp.float32
    bsz, L, _ = q.shape
    nb = -(-L // ATTN_BLOCK)
    lp = nb * ATTN_BLOCK
    q = q.reshape(bsz, L, SWA_KV_HEADS, SWA_GROUP, SWA_HEAD_DIM) * SWA_HEAD_DIM ** -0.5
    k = k.reshape(bsz, L, SWA_KV_HEADS, SWA_HEAD_DIM)
    v = v.reshape(bsz, L, SWA_KV_HEADS, SWA_HEAD_DIM)
    k_meta, v_meta = k[:, :N_META], v[:, :N_META]
    qb = jnp.pad(q, ((0, 0), (0, lp - L), (0, 0), (0, 0), (0, 0)))
    qb = jnp.moveaxis(qb.reshape(bsz, nb, ATTN_BLOCK, SWA_KV_HEADS, SWA_GROUP, SWA_HEAD_DIM), 1, 0)

    def band(t):
        tp = jnp.pad(t, ((0, 0), (ATTN_BLOCK, lp - L), (0, 0), (0, 0)))
        tp = tp.reshape(bsz, nb + 1, ATTN_BLOCK, SWA_KV_HEADS, SWA_HEAD_DIM)
        return jnp.moveaxis(jnp.concatenate([tp[:, :-1], tp[:, 1:]], axis=2), 1, 0)

    kb, vb = band(k), band(v)
    slopes = jnp.exp2(-8.0 * jnp.arange(1, SWA_Q_HEADS + 1, dtype=f32) / SWA_Q_HEADS)
    slopes = slopes.reshape(SWA_KV_HEADS, SWA_GROUP)[:, :, None, None]
    sink = sinks.astype(f32).reshape(SWA_KV_HEADS, SWA_GROUP)[:, :, None, None]
    s_meta = jnp.arange(N_META)

    def block_attn(args):
        qi, ki, vi, bidx = args
        t = bidx * ATTN_BLOCK + jnp.arange(ATTN_BLOCK)
        s_band = (bidx - 1) * ATTN_BLOCK + jnp.arange(2 * ATTN_BLOCK)
        d_band = t[:, None] - s_band[None, :]
        d_meta = t[:, None] - s_meta[None, :]
        m_band = (s_band[None, :] >= N_META) & (d_band >= 0) & (d_band < WINDOW)
        m_meta = d_meta >= 0
        sc_band = jnp.einsum('bqhgd,bshd->bhgqs', qi, ki).astype(f32) - slopes * d_band.astype(f32)
        sc_meta = jnp.einsum('bqhgd,bshd->bhgqs', qi, k_meta).astype(f32) - slopes * d_meta.astype(f32)
        sc_band = jnp.where(m_band, sc_band, -jnp.inf)
        sc_meta = jnp.where(m_meta, sc_meta, -jnp.inf)
        sink_col = jnp.broadcast_to(sink, sc_band.shape[:-1] + (1,))
        p = jax.nn.softmax(jnp.concatenate([sc_meta, sc_band, sink_col], -1), axis=-1)
        p_meta = p[..., :N_META].astype(vi.dtype)
        p_band = p[..., N_META:N_META + 2 * ATTN_BLOCK].astype(vi.dtype)
        return (jnp.einsum('bhgqs,bshd->bqhgd', p_meta, v_meta)
                + jnp.einsum('bhgqs,bshd->bqhgd', p_band, vi))

    o = lax.map(block_attn, (qb, kb, vb, jnp.arange(nb)))
    o = jnp.moveaxis(o, 0, 1).reshape(bsz, lp, SWA_QD)[:, :L]
    return o


def moe_ffn(x, w_router, b_router, w1, b1, w2, b2):
    n_tok = x.shape[0]
    logits = (x @ w_router + b_router).astype(jnp.float32)
    top_logit, top_idx = lax.top_k(logits, TOP_K)
    top_w = jax.nn.softmax(top_logit, axis=-1)
    n_asg = n_tok * TOP_K
    flat_e = top_idx.reshape(-1)
    flat_tok = jnp.arange(n_asg, dtype=jnp.int32) // TOP_K
    order = jnp.argsort(flat_e)
    se, st, sw = flat_e[order], flat_tok[order], top_w.reshape(-1)[order]
    counts = jnp.zeros((N_EXPERTS,), jnp.int32).at[flat_e].add(1)
    padded = (counts + MOE_BLOCK - 1) // MOE_BLOCK * MOE_BLOCK
    grp_start = jnp.cumsum(counts) - counts
    pad_end = jnp.cumsum(padded)
    pad_start = pad_end - padded
    dest = pad_start[se] + (jnp.arange(n_asg, dtype=jnp.int32) - grp_start[se])
    n_blk = -(-n_asg // MOE_BLOCK) + N_EXPERTS
    n_rows = n_blk * MOE_BLOCK
    row_tok = jnp.full((n_rows,), n_tok, jnp.int32).at[dest].set(st)
    row_w = jnp.zeros((n_rows,), jnp.float32).at[dest].set(sw)
    x_pad = jnp.concatenate([x, jnp.zeros((1, x.shape[1]), x.dtype)], 0)
    xs = x_pad[row_tok].reshape(n_blk, MOE_BLOCK, x.shape[1])
    blk_e = jnp.minimum(jnp.searchsorted(pad_end, jnp.arange(n_blk) * MOE_BLOCK, side='right'), N_EXPERTS - 1)

    def expert_block(args):
        xb, e = args
        hdn = xb @ w1[e] + b1[e]
        gate, up = hdn[:, :D_FF], hdn[:, D_FF:]
        gate = jnp.minimum(gate, SWIGLU_LIMIT)
        up = jnp.clip(up, -SWIGLU_LIMIT, SWIGLU_LIMIT)
        return (gate * jax.nn.sigmoid(SWIGLU_ALPHA * gate) * (up + 1.0)) @ w2[e] + b2[e]

    ys = lax.map(expert_block, (xs, blk_e)).reshape(n_rows, x.shape[1])
    ys = (ys * row_w[:, None]).astype(x.dtype)
    return jax.ops.segment_sum(ys, row_tok, num_segments=n_tok + 1)[:n_tok]


def setup_inputs(seed: int = 0) -> dict:
    key = jax.random.key(seed)
    ks = jax.random.split(key, 24)
    f32 = jnp.float32
    nrm = lambda k, shape, s: jax.random.normal(k, shape, f32) * s
    dt = jnp.exp(jax.random.uniform(ks[6], (DEPTH, GDN_HEADS), f32, np.log(1e-3), np.log(1e-1)))
    return {
        'x': nrm(ks[0], (BATCH, SEQ, D_MODEL), 1.0),
        'meta_tokens': nrm(ks[1], (N_META, D_MODEL), 1.0),
        'emb_ln_g': 1.0 + nrm(ks[2], (D_MODEL,), 0.02),
        'emb_ln_b': nrm(ks[3], (D_MODEL,), 0.02),
        'w_in': nrm(ks[4], (DEPTH, D_MODEL, C_IN), D_MODEL ** -0.5),
        'conv_w': nrm(ks[5], (DEPTH, 2 * GDN_QK + GDN_VD, CONV_WIDTH), CONV_WIDTH ** -0.5),
        'a_log': jnp.log(jax.random.uniform(ks[7], (DEPTH, GDN_HEADS), f32, 1.0, 16.0)),
        'dt_bias': dt + jnp.log(-jnp.expm1(-dt)),
        'gdn_norm_w': 1.0 + nrm(ks[8], (DEPTH, GDN_DV), 0.02),
        'attn_sinks': nrm(ks[9], (DEPTH, SWA_Q_HEADS), 0.5),
        'w_br_gdn': nrm(ks[10], (DEPTH, GDN_VD, D_MODEL), GDN_VD ** -0.5),
        'w_br_swa': nrm(ks[11], (DEPTH, SWA_QD, D_MODEL), SWA_QD ** -0.5),
        'w_out': nrm(ks[12], (DEPTH, D_MODEL, D_MODEL), D_MODEL ** -0.5 * DEEPNORM_BETA),
        'ln1_g': 1.0 + nrm(ks[13], (DEPTH, D_MODEL), 0.02),
        'ln1_b': nrm(ks[14], (DEPTH, D_MODEL), 0.02),
        'w_router': nrm(ks[15], (DEPTH, D_MODEL, N_EXPERTS), D_MODEL ** -0.5),
        'b_router': nrm(ks[16], (DEPTH, N_EXPERTS), 0.01),
        'w_moe1': nrm(ks[17], (DEPTH, N_EXPERTS, D_MODEL, 2 * D_FF), D_MODEL ** -0.5),
        'b_moe1': nrm(ks[18], (DEPTH, N_EXPERTS, 2 * D_FF), 0.02),
        'w_moe2': nrm(ks[19], (DEPTH, N_EXPERTS, D_FF, D_MODEL), D_FF ** -0.5 * DEEPNORM_BETA),
        'b_moe2': nrm(ks[20], (DEPTH, N_EXPERTS, D_MODEL), 0.02),
        'ln2_g': 1.0 + nrm(ks[21], (DEPTH, D_MODEL), 0.02),
        'ln2_b': nrm(ks[22], (DEPTH, D_MODEL), 0.02),
    }


def reference(x, meta_tokens, emb_ln_g, emb_ln_b, w_in, conv_w, a_log, dt_bias, gdn_norm_w, attn_sinks,
              w_br_gdn, w_br_swa, w_out, ln1_g, ln1_b, w_router, b_router, w_moe1, b_moe1, w_moe2, b_moe2,
              ln2_g, ln2_b):
    bsz = x.shape[0]
    meta = jnp.broadcast_to(meta_tokens[None].astype(x.dtype), (bsz, N_META, D_MODEL))
    h = layer_norm(jnp.concatenate([meta, x], axis=1), emb_ln_g, emb_ln_b)
    for l in range(DEPTH):
        proj = h @ w_in[l]
        q_g, k_g, v_g, z_g, b_g, a_g, q_s, k_s, v_s, gate_g, gate_s = jnp.split(proj, SPLIT_POINTS, axis=-1)
        y_gdn = gated_deltanet(q_g, k_g, v_g, z_g, b_g, a_g, conv_w[l], a_log[l], dt_bias[l], gdn_norm_w[l]) @ w_br_gdn[l]
        y_swa = sliding_window_attention(q_s, k_s, v_s, attn_sinks[l]) @ w_br_swa[l]
        mixed = (jax.nn.sigmoid(gate_g) * y_gdn + jax.nn.sigmoid(gate_s) * y_swa) @ w_out[l]
        h = layer_norm(DEEPNORM_ALPHA * h + mixed, ln1_g[l], ln1_b[l])
        if l == DEPTH - 1:
            h = h[:, N_META:]
        n_pos = h.shape[1]
        ffn = moe_ffn(h.reshape(-1, D_MODEL), w_router[l], b_router[l], w_moe1[l], b_moe1[l],
                      w_moe2[l], b_moe2[l]).reshape(bsz, n_pos, D_MODEL)
        h = layer_norm(DEEPNORM_ALPHA * h + ffn, ln2_g[l], ln2_b[l])
    return h
```

```python
import functools

import jax
import jax.numpy as jnp
import numpy as np
from jax import lax
from jax.experimental import pallas as pl
from jax.experimental.pallas import tpu as pltpu

N_META = 16
GDN_HEADS = 8
GDN_DK = 128
GDN_DV = 128
CONV_WIDTH = 4
CHUNK = 64
SWA_Q_HEADS = 16
SWA_KV_HEADS = 4
SWA_HEAD_DIM = 64
SWA_GROUP = SWA_Q_HEADS // SWA_KV_HEADS
WINDOW = 128
N_EXPERTS = 32
TOP_K = 4
SWIGLU_ALPHA = 1.702
SWIGLU_LIMIT = 7.0
LN_EPS = 1e-5
RMS_EPS = 1e-6
DEPTH = 1
DEEPNORM_ALPHA = (2.0 * DEPTH) ** 0.25

GDN_QK = GDN_HEADS * GDN_DK
GDN_VD = GDN_HEADS * GDN_DV
SWA_QD = SWA_Q_HEADS * SWA_HEAD_DIM
SWA_KVD = SWA_KV_HEADS * SWA_HEAD_DIM

LANES = 128
MOE_TILE = 512
GATHER_TOKENS = 256
NEG_BIG = -1e30
VMEM_LIMIT = 56 * 1024 * 1024

F32 = jnp.float32
BF16 = jnp.bfloat16
HIGHEST = lax.Precision.HIGHEST


def _layer_norm(x, g, b):
    mu = jnp.mean(x, -1, keepdims=True)
    xc = x - mu
    var = jnp.mean(xc * xc, -1, keepdims=True)
    return xc * lax.rsqrt(var + LN_EPS) * g + b


def _sigmoid(x):
    return 1.0 / (1.0 + jnp.exp(-x))


def _silu(x):
    return x * _sigmoid(x)


def _softplus(x):
    return jnp.maximum(x, 0.0) + jnp.log(1.0 + jnp.exp(-jnp.abs(x)))


def _dot(a, b, **kw):
    return jnp.dot(a, b, preferred_element_type=F32, **kw)


def _dot_nt(a, b):
    return lax.dot_general(a, b, (((1,), (1,)), ((), ())), preferred_element_type=F32)


def _pack_rows(y):
    n = y.shape[1] // 2
    lo = pltpu.bitcast(y[:, :n].astype(BF16).astype(F32), jnp.uint32)
    hi = pltpu.bitcast(y[:, n:].astype(BF16).astype(F32), jnp.uint32)
    return (lo >> 16) | hi


def _unpack_rows(w):
    lo = pltpu.bitcast(w << 16, F32)
    hi = pltpu.bitcast(w & jnp.uint32(0xFFFF0000), F32)
    return lo, hi


_C_QKVG = 2 * GDN_QK + GDN_VD
_C_Z = _C_QKVG + GDN_VD
_C_QS = _C_Z + SWA_QD
_C_KVS = _C_QS + 2 * SWA_KVD
_C_GATES = _C_KVS + 2 * GDN_VD
_C_BA = _C_GATES + LANES


def _inproj_kernel(x_ref, g_ref, b_ref, w_ref, qkvg_ref, z_ref, qs_ref, kvs_ref, gates_ref, ba_ref):
    h = _layer_norm(x_ref[...], g_ref[...], b_ref[...]).astype(BF16)

    def proj(out_ref, lo, hi, step=1024):
        for c in range(lo, hi, step):
            ce = min(c + step, hi)
            out_ref[:, c - lo:ce - lo] = _dot(h, w_ref[:, c:ce]).astype(out_ref.dtype)

    proj(qkvg_ref, 0, _C_QKVG)
    proj(z_ref, _C_QKVG, _C_Z)
    proj(qs_ref, _C_Z, _C_QS)
    proj(kvs_ref, _C_QS, _C_KVS)
    proj(gates_ref, _C_KVS, _C_GATES)
    proj(ba_ref, _C_GATES, _C_BA)


def _in_proj(x2d, ln_g, ln_b, w_perm, tm):
    n, d = x2d.shape
    widths = (_C_QKVG, GDN_VD, SWA_QD, 2 * SWA_KVD, _C_GATES - _C_KVS, LANES)
    dtypes = (BF16, BF16, BF16, BF16, BF16, F32)
    row = lambda i: (i, 0)
    const = lambda i: (0, 0)
    return pl.pallas_call(
        _inproj_kernel,
        grid=(n // tm,),
        in_specs=[
            pl.BlockSpec((tm, d), row),
            pl.BlockSpec((1, d), const),
            pl.BlockSpec((1, d), const),
            pl.BlockSpec((d, _C_BA), const, pipeline_mode=pl.Buffered(1)),
        ],
        out_specs=[pl.BlockSpec((tm, w), row) for w in widths],
        out_shape=[jax.ShapeDtypeStruct((n, w), dt) for w, dt in zip(widths, dtypes)],
        compiler_params=pltpu.CompilerParams(dimension_semantics=("parallel",), vmem_limit_bytes=VMEM_LIMIT),
        name="in_proj",
    )(x2d, ln_g, ln_b, w_perm)


_HIST = 8


def _gdn_kernel(qkv_ref, z_ref, ba_ref, at_ref, hist_ref, s0_ref, convw_ref, alog_r_ref, dtb_r_ref,
                alog_c_ref, dtb_c_ref, normw_ref, *rest, n_pad, emit_state):
    if emit_state:
        s_out_ref, s_ref, win_ref = rest
    else:
        o_ref, s_ref, win_ref = rest
    n = pl.program_id(1)
    c = CHUNK

    @pl.when(n == 0)
    def _():
        s_ref[...] = s0_ref[...]
        win_ref[0:_HIST, :] = hist_ref[...].astype(F32)

    win_ref[_HIST:_HIST + c, :] = qkv_ref[...].astype(F32)
    acc = win_ref[_HIST:_HIST + c, :] * convw_ref[CONV_WIDTH - 1:CONV_WIDTH, :]
    for i in range(CONV_WIDTH - 1):
        sh = CONV_WIDTH - 1 - i
        acc = acc + win_ref[_HIST - sh:_HIST - sh + c, :] * convw_ref[i:i + 1, :]
    win_ref[0:_HIST, :] = win_ref[c:c + _HIST, :]
    qkv = _silu(acc)

    row = lax.broadcasted_iota(jnp.int32, (c, c), 0)
    col = lax.broadcasted_iota(jnp.int32, (c, c), 1)
    incl = row >= col
    strict = row > col
    tri_incl = incl.astype(F32)
    tri_upper = (row <= col).astype(F32)
    eye = (row == col).astype(F32)

    beta = _sigmoid(ba_ref[:, 0:GDN_HEADS])
    log_g_c = -jnp.exp(alog_r_ref[...]) * _softplus(ba_ref[:, GDN_HEADS:2 * GDN_HEADS] + dtb_r_ref[...])
    log_g_r = -jnp.exp(alog_c_ref[...]) * _softplus(at_ref[...] + dtb_c_ref[...])
    if n_pad:
        valid_c = lax.broadcasted_iota(jnp.int32, (c, GDN_HEADS), 0) >= n_pad
        valid_r = lax.broadcasted_iota(jnp.int32, (GDN_HEADS, c), 1) >= n_pad
        beta = jnp.where(valid_c, beta, 0.0)
        log_g_c = jnp.where(valid_c, log_g_c, 0.0)
        log_g_r = jnp.where(valid_r, log_g_r, 0.0)
    gam_c = _dot(tri_incl, log_g_c, precision=HIGHEST)
    gam_r = _dot(log_g_r, tri_upper, precision=HIGHEST)

    for h in range(GDN_HEADS):
        qh = qkv[:, h * GDN_DK:(h + 1) * GDN_DK]
        kh = qkv[:, GDN_QK + h * GDN_DK:GDN_QK + (h + 1) * GDN_DK]
        vh = qkv[:, 2 * GDN_QK + h * GDN_DV:2 * GDN_QK + (h + 1) * GDN_DV]
        qn = qh * lax.rsqrt(jnp.sum(qh * qh, -1, keepdims=True) + RMS_EPS) * (GDN_DK ** -0.5)
        kn = kh * lax.rsqrt(jnp.sum(kh * kh, -1, keepdims=True) + RMS_EPS)
        b_c = beta[:, h:h + 1]
        g_c = gam_c[:, h:h + 1]
        g_r = gam_r[h:h + 1, :]
        g_last = g_c[c - 1:c, :]
        decay = jnp.where(incl, jnp.exp(jnp.where(incl, g_c - g_r, 0.0)), 0.0)
        kn_b = kn.astype(BF16)
        kk = _dot_nt(kn_b, kn_b)
        a_mat = jnp.where(strict, kk * decay, 0.0) * b_c
        t_inv = eye - a_mat
        a_pow = a_mat
        for _ in range(5):
            a_pow = _dot(a_pow, a_pow, precision=HIGHEST)
            t_inv = t_inv + _dot(t_inv, a_pow, precision=HIGHEST)
        e_g = jnp.exp(g_c)
        rhs = jnp.concatenate([vh * b_c, kn * (b_c * e_g)], axis=-1)
        sol = _dot(t_inv, rhs, precision=HIGHEST)
        u_v, w = sol[:, :GDN_DV], sol[:, GDN_DV:]
        qk = _dot_nt(qn.astype(BF16), kn_b) * decay
        q_dec = qn * e_g
        k_dec = kn * jnp.exp(g_last - g_c)
        state = s_ref[h]
        state_b = state.astype(BF16)
        u = u_v - _dot(w.astype(BF16), state_b)
        u_b = u.astype(BF16)
        o = _dot(q_dec.astype(BF16), state_b) + _dot(qk.astype(BF16), u_b)
        s_ref[h] = state * jnp.exp(g_last) + _dot(k_dec.T.astype(BF16), u_b)
        if not emit_state:
            o = o * lax.rsqrt(jnp.mean(o * o, -1, keepdims=True) + RMS_EPS) * normw_ref[...]
            zh = z_ref[:, h * GDN_DV:(h + 1) * GDN_DV].astype(F32)
            o_ref[:, h * GDN_DV:(h + 1) * GDN_DV] = (o * _silu(zh)).astype(o_ref.dtype)

    if emit_state:
        s_out_ref[...] = s_ref[...]


def _gdn(qkv, z, ba, a_t, hist, s0, conv_w_t, a_log, dt_bias, norm_w, bsz, n_chunks, n_pad, emit_state):
    c = CHUNK
    rows = lambda b, n: (b * n_chunks + n, 0)
    c2 = lambda b, n: (0, 0)
    c3 = lambda b, n: (0, 0, 0)
    in_specs = [
        pl.BlockSpec((c, qkv.shape[1]), rows),
        pl.BlockSpec((c, z.shape[1]), rows),
        pl.BlockSpec((c, ba.shape[1]), rows),
        pl.BlockSpec((None, GDN_HEADS, c), lambda b, n: (b * n_chunks + n, 0, 0)),
        pl.BlockSpec(hist.shape, c2),
        pl.BlockSpec(s0.shape, c3),
        pl.BlockSpec(conv_w_t.shape, c2),
        pl.BlockSpec((1, GDN_HEADS), c2),
        pl.BlockSpec((1, GDN_HEADS), c2),
        pl.BlockSpec((GDN_HEADS, 1), c2),
        pl.BlockSpec((GDN_HEADS, 1), c2),
        pl.BlockSpec((1, GDN_DV), c2),
    ]
    if emit_state:
        out_specs = pl.BlockSpec(s0.shape, c3)
        out_shape = jax.ShapeDtypeStruct(s0.shape, F32)
    else:
        out_specs = pl.BlockSpec((c, GDN_VD), rows)
        out_shape = jax.ShapeDtypeStruct((qkv.shape[0], GDN_VD), BF16)
    return pl.pallas_call(
        functools.partial(_gdn_kernel, n_pad=n_pad, emit_state=emit_state),
        grid=(bsz, n_chunks),
        in_specs=in_specs,
        out_specs=out_specs,
        out_shape=out_shape,
        scratch_shapes=[
            pltpu.VMEM((GDN_HEADS, GDN_DK, GDN_DV), F32),
            pltpu.VMEM((_HIST + c, qkv.shape[1]), F32),
        ],
        compiler_params=pltpu.CompilerParams(dimension_semantics=("parallel", "arbitrary"),
                                             vmem_limit_bytes=VMEM_LIMIT),
        name="gdn_meta" if emit_state else "gdn",
    )(qkv, z, ba, a_t, hist, s0, conv_w_t, a_log.reshape(1, -1), dt_bias.reshape(1, -1),
      a_log.reshape(-1, 1), dt_bias.reshape(-1, 1), norm_w.reshape(1, -1))


def _swa_kernel(q_ref, kvc_ref, kvp_ref, kvm_ref, sink_ref, o_ref):
    i = pl.program_id(1)
    blk = WINDOW
    hd = SWA_HEAD_DIM
    t = lax.broadcasted_iota(jnp.int32, (blk, blk), 0)
    s = lax.broadcasted_iota(jnp.int32, (blk, blk), 1)
    d_cur = (t - s).astype(F32)
    d_prev = d_cur + float(blk)
    m_cur = t >= s
    m_prev = jnp.logical_and(s > t, i > 0)
    tm_ = lax.broadcasted_iota(jnp.int32, (blk, N_META), 0)
    sm_ = lax.broadcasted_iota(jnp.int32, (blk, N_META), 1)
    d_meta = (tm_ - sm_ + N_META + i * blk).astype(F32)

    for g in range(SWA_KV_HEADS):
        k_cur = kvc_ref[:, g * hd:(g + 1) * hd]
        v_cur = kvc_ref[:, SWA_KVD + g * hd:SWA_KVD + (g + 1) * hd]
        k_prev = kvp_ref[:, g * hd:(g + 1) * hd]
        v_prev = kvp_ref[:, SWA_KVD + g * hd:SWA_KVD + (g + 1) * hd]
        k_meta = kvm_ref[:, g * hd:(g + 1) * hd]
        v_meta = kvm_ref[:, SWA_KVD + g * hd:SWA_KVD + (g + 1) * hd]
        for j in range(SWA_GROUP):
            hq = g * SWA_GROUP + j
            slope = float(2.0 ** (-8.0 * (hq + 1) / SWA_Q_HEADS))
            qh = q_ref[:, hq * hd:(hq + 1) * hd] * (hd ** -0.5)
            sc_cur = jnp.where(m_cur, _dot_nt(qh, k_cur) - slope * d_cur, NEG_BIG)
            sc_prev = jnp.where(m_prev, _dot_nt(qh, k_prev) - slope * d_prev, NEG_BIG)
            sc_meta = _dot_nt(qh, k_meta) - slope * d_meta
            sink = sink_ref[:, hq:hq + 1]
            m = jnp.maximum(jnp.maximum(jnp.max(sc_cur, -1, keepdims=True), jnp.max(sc_prev, -1, keepdims=True)),
                            jnp.maximum(jnp.max(sc_meta, -1, keepdims=True), sink))
            e_cur = jnp.exp(sc_cur - m)
            e_prev = jnp.exp(sc_prev - m)
            e_meta = jnp.exp(sc_meta - m)
            denom = (jnp.sum(e_cur, -1, keepdims=True) + jnp.sum(e_prev, -1, keepdims=True)
                     + jnp.sum(e_meta, -1, keepdims=True) + jnp.exp(sink - m))
            o = (_dot(e_cur.astype(BF16), v_cur) + _dot(e_prev.astype(BF16), v_prev)
                 + _dot(e_meta.astype(BF16), v_meta))
            o_ref[:, hq * hd:(hq + 1) * hd] = (o / denom).astype(o_ref.dtype)


def _swa(q, kv, kv_meta, sinks, bsz, nq):
    blk = WINDOW
    return pl.pallas_call(
        _swa_kernel,
        grid=(bsz, nq),
        in_specs=[
            pl.BlockSpec((blk, SWA_QD), lambda b, i: (b * nq + i, 0)),
            pl.BlockSpec((blk, 2 * SWA_KVD), lambda b, i: (b * nq + i, 0)),
            pl.BlockSpec((blk, 2 * SWA_KVD), lambda b, i: (b * nq + jnp.maximum(i - 1, 0), 0)),
            pl.BlockSpec((N_META, 2 * SWA_KVD), lambda b, i: (0, 0)),
            pl.BlockSpec((1, SWA_Q_HEADS), lambda b, i: (0, 0)),
        ],
        out_specs=pl.BlockSpec((blk, SWA_QD), lambda b, i: (b * nq + i, 0)),
        out_shape=jax.ShapeDtypeStruct((q.shape[0], SWA_QD), BF16),
        compiler_params=pltpu.CompilerParams(dimension_semantics=("parallel", "parallel"),
                                             vmem_limit_bytes=VMEM_LIMIT),
        name="swa",
    )(q, kv, kv, kv_meta, sinks.reshape(1, -1))


def _merge_kernel(x_ref, eg_ref, eb_ref, og_ref, os_ref, gates_ref, wg_ref, ws_ref, wo_ref, g1_ref, b1_ref,
                  wr_ref, br_ref, h1_ref, h1p_ref, idx_ref, tw_ref):
    d = x_ref.shape[1]
    h0 = _layer_norm(x_ref[...], eg_ref[...], eb_ref[...])
    y_g = _dot(og_ref[...], wg_ref[...])
    y_s = _dot(os_ref[...], ws_ref[...])
    mix = _sigmoid(gates_ref[:, :d].astype(F32)) * y_g + _sigmoid(gates_ref[:, d:].astype(F32)) * y_s
    mixed = _dot(mix.astype(BF16), wo_ref[...])
    h1 = _layer_norm(DEEPNORM_ALPHA * h0 + mixed, g1_ref[...], b1_ref[...])
    h1_ref[...] = h1
    h1p_ref[...] = _pack_rows(h1)

    logits = _dot(h1, wr_ref[...], precision=HIGHEST) + br_ref[...]
    lane = lax.broadcasted_iota(jnp.int32, logits.shape, 1)
    vals, idxs = [], []
    for _ in range(TOP_K):
        m = jnp.max(logits, -1, keepdims=True)
        sel = jnp.min(jnp.where(logits == m, lane, LANES), -1, keepdims=True)
        vals.append(m)
        idxs.append(sel)
        logits = jnp.where(lane == sel, NEG_BIG, logits)
    e = [jnp.exp(v - vals[0]) for v in vals]
    tot = e[0] + e[1] + e[2] + e[3]
    for k in range(TOP_K):
        idx_ref[:, k:k + 1] = idxs[k]
        tw_ref[:, k:k + 1] = e[k] / tot


def _merge(x2d, eg, eb, o_gdn, o_swa, gates, wg, ws, wo, g1, b1, wr, br, tm):
    n, d = x2d.shape
    row = lambda i: (i, 0)
    const = lambda i: (0, 0)
    full = lambda a: pl.BlockSpec(a.shape, const)
    return pl.pallas_call(
        _merge_kernel,
        grid=(n // tm,),
        in_specs=[
            pl.BlockSpec((tm, d), row), full(eg), full(eb),
            pl.BlockSpec((tm, GDN_VD), row), pl.BlockSpec((tm, SWA_QD), row), pl.BlockSpec((tm, 2 * d), row),
            full(wg), full(ws), full(wo), full(g1), full(b1), full(wr), full(br),
        ],
        out_specs=[
            pl.BlockSpec((tm, d), row), pl.BlockSpec((tm, d // 2), row),
            pl.BlockSpec((tm, TOP_K), row), pl.BlockSpec((tm, TOP_K), row),
        ],
        out_shape=[
            jax.ShapeDtypeStruct((n, d), F32), jax.ShapeDtypeStruct((n, d // 2), jnp.uint32),
            jax.ShapeDtypeStruct((n, TOP_K), jnp.int32), jax.ShapeDtypeStruct((n, TOP_K), F32),
        ],
        compiler_params=pltpu.CompilerParams(dimension_semantics=("parallel",), vmem_limit_bytes=VMEM_LIMIT),
        name="merge",
    )(x2d, eg, eb, o_gdn, o_swa, gates, wg, ws, wo, g1, b1, wr, br)


def _rank_kernel(idx_ref, rank_ref, counts_ref, run_ref):
    t = idx_ref.shape[0]

    @pl.when(pl.program_id(0) == 0)
    def _():
        run_ref[...] = jnp.zeros_like(run_ref)

    lane = lax.broadcasted_iota(jnp.int32, (t, LANES), 1)
    hits = [idx_ref[:, k:k + 1] == lane for k in range(TOP_K)]
    onehot = hits[0].astype(F32)
    for k in range(1, TOP_K):
        onehot = onehot + hits[k].astype(F32)
    r = lax.broadcasted_iota(jnp.int32, (t, t), 0)
    c = lax.broadcasted_iota(jnp.int32, (t, t), 1)
    before = (r > c).astype(BF16)
    base = run_ref[...] + _dot(before, onehot.astype(BF16))
    for k in range(TOP_K):
        rank_ref[:, k:k + 1] = jnp.sum(jnp.where(hits[k], base, 0.0), -1, keepdims=True).astype(jnp.int32)
    run_ref[...] = run_ref[...] + jnp.sum(onehot, 0, keepdims=True)
    counts_ref[...] = run_ref[...].astype(jnp.int32)


def _rank(top_idx, t):
    n = top_idx.shape[0]
    return pl.pallas_call(
        _rank_kernel,
        grid=(n // t,),
        in_specs=[pl.BlockSpec((t, TOP_K), lambda i: (i, 0))],
        out_specs=[pl.BlockSpec((t, TOP_K), lambda i: (i, 0)), pl.BlockSpec((1, LANES), lambda i: (0, 0))],
        out_shape=[jax.ShapeDtypeStruct((n, TOP_K), jnp.int32), jax.ShapeDtypeStruct((1, LANES), jnp.int32)],
        scratch_shapes=[pltpu.VMEM((1, LANES), F32)],
        compiler_params=pltpu.CompilerParams(dimension_semantics=("arbitrary",)),
        name="rank",
    )(top_idx)


def _row_copy(src, dst, sem):
    return pltpu.make_async_copy(src, dst, sem)


def _dispatch_kernel(dest_hbm, h_ref, xs_in, xs_hbm, idx_smem, sem_idx, sem_rows):
    del xs_in
    i = pl.program_id(0)
    t = h_ref.shape[0]
    n_idx = t * TOP_K
    idx_cp = _row_copy(dest_hbm.at[pl.ds(pl.multiple_of(i * n_idx, n_idx), n_idx)], idx_smem, sem_idx)
    idx_cp.start()
    idx_cp.wait()

    def issue(a, carry):
        _row_copy(h_ref.at[pl.ds(a // TOP_K, 1)], xs_hbm.at[pl.ds(idx_smem[a], 1)], sem_rows).start()
        return carry

    lax.fori_loop(0, n_idx, issue, 0)

    def drain(a, carry):
        _row_copy(h_ref.at[pl.ds(0, 1)], xs_hbm.at[pl.ds(0, 1)], sem_rows).wait()
        return carry

    lax.fori_loop(0, n_idx, drain, 0)


def _dispatch(dest_flat, h1p, xs_init, t):
    n, w = h1p.shape
    return pl.pallas_call(
        _dispatch_kernel,
        grid=(n // t,),
        in_specs=[
            pl.BlockSpec(memory_space=pl.ANY),
            pl.BlockSpec((t, w), lambda i: (i, 0)),
            pl.BlockSpec(memory_space=pl.ANY),
        ],
        out_specs=pl.BlockSpec(memory_space=pl.ANY),
        out_shape=jax.ShapeDtypeStruct(xs_init.shape, xs_init.dtype),
        scratch_shapes=[pltpu.SMEM((t * TOP_K,), jnp.int32), pltpu.SemaphoreType.DMA, pltpu.SemaphoreType.DMA],
        input_output_aliases={2: 0},
        compiler_params=pltpu.CompilerParams(dimension_semantics=("arbitrary",)),
        name="dispatch",
    )(dest_flat, h1p, xs_init)


def _expert_kernel(blk_e_ref, n_used_ref, xs_ref, w1_ref, b1_ref, w2_ref, b2_ref, ys_ref):
    del blk_e_ref
    i = pl.program_id(0)
    d_ff = w2_ref.shape[0]

    @pl.when(i < n_used_ref[0])
    def _():
        lo, hi = _unpack_rows(xs_ref[...])
        xb = jnp.concatenate([lo, hi], axis=-1).astype(BF16)
        hdn = _dot(xb, w1_ref[...]) + b1_ref[...]
        gate = jnp.minimum(hdn[:, :d_ff], SWIGLU_LIMIT)
        up = jnp.clip(hdn[:, d_ff:], -SWIGLU_LIMIT, SWIGLU_LIMIT)
        act = gate * _sigmoid(SWIGLU_ALPHA * gate) * (up + 1.0)
        y = _dot(act.astype(BF16), w2_ref[...]) + b2_ref[...]
        ys_ref[...] = _pack_rows(y)

    @pl.when(i >= n_used_ref[0])
    def _():
        ys_ref[...] = jnp.zeros_like(ys_ref)


def _experts(blk_e, n_used, xs, w1, b1, w2, b2):
    n_rows, w = xs.shape
    n_e, d, d_ff2 = w1.shape
    d_ff = w2.shape[1]
    grid_spec = pltpu.PrefetchScalarGridSpec(
        num_scalar_prefetch=2,
        grid=(n_rows // MOE_TILE,),
        in_specs=[
            pl.BlockSpec((MOE_TILE, w), lambda i, be, nu: (i, 0)),
            pl.BlockSpec((None, d, d_ff2), lambda i, be, nu: (be[i], 0, 0)),
            pl.BlockSpec((None, 1, d_ff2), lambda i, be, nu: (be[i], 0, 0)),
            pl.BlockSpec((None, d_ff, d), lambda i, be, nu: (be[i], 0, 0)),
            pl.BlockSpec((None, 1, d), lambda i, be, nu: (be[i], 0, 0)),
        ],
        out_specs=pl.BlockSpec((MOE_TILE, w), lambda i, be, nu: (i, 0)),
    )
    return pl.pallas_call(
        _expert_kernel,
        grid_spec=grid_spec,
        out_shape=jax.ShapeDtypeStruct((n_rows, w), jnp.uint32),
        compiler_params=pltpu.CompilerParams(dimension_semantics=("arbitrary",), vmem_limit_bytes=VMEM_LIMIT),
        name="experts",
    )(blk_e, n_used, xs, w1, b1.reshape(n_e, 1, d_ff2), w2, b2.reshape(n_e, 1, d))


def _combine_kernel(dest_hbm, ys_hbm, h1_ref, tw_ref, g_ref, b_ref, out_ref, idx_smem, buf, sem_idx, sem_rows):
    i = pl.program_id(0)
    t = h1_ref.shape[0]
    n_idx = t * TOP_K
    idx_cp = _row_copy(dest_hbm.at[pl.ds(pl.multiple_of(i * n_idx, n_idx), n_idx)], idx_smem, sem_idx)
    idx_cp.start()
    idx_cp.wait()

    def issue(a, carry):
        _row_copy(ys_hbm.at[pl.ds(idx_smem[a], 1)], buf.at[a % TOP_K, pl.ds(a // TOP_K, 1)], sem_rows).start()
        return carry

    lax.fori_loop(0, n_idx, issue, 0)

    def drain(a, carry):
        _row_copy(ys_hbm.at[pl.ds(0, 1)], buf.at[0, pl.ds(0, 1)], sem_rows).wait()
        return carry

    lax.fori_loop(0, n_idx, drain, 0)

    ffn = None
    for k in range(TOP_K):
        lo, hi = _unpack_rows(buf[k])
        term = jnp.concatenate([lo, hi], axis=-1) * tw_ref[:, k:k + 1]
        ffn = term if ffn is None else ffn + term
    out_ref[...] = _layer_norm(DEEPNORM_ALPHA * h1_ref[...] + ffn, g_ref[...], b_ref[...])


def _combine(dest_flat, ys, h1, top_w, g2, b2, t):
    n, d = h1.shape
    w = ys.shape[1]
    return pl.pallas_call(
        _combine_kernel,
        grid=(n // t,),
        in_specs=[
            pl.BlockSpec(memory_space=pl.ANY),
            pl.BlockSpec(memory_space=pl.ANY),
            pl.BlockSpec((t, d), lambda i: (i, 0)),
            pl.BlockSpec((t, TOP_K), lambda i: (i, 0)),
            pl.BlockSpec((1, d), lambda i: (0, 0)),
            pl.BlockSpec((1, d), lambda i: (0, 0)),
        ],
        out_specs=pl.BlockSpec((t, d), lambda i: (i, 0)),
        out_shape=jax.ShapeDtypeStruct((n, d), F32),
        scratch_shapes=[
            pltpu.SMEM((t * TOP_K,), jnp.int32),
            pltpu.VMEM((TOP_K, t, w), jnp.uint32),
            pltpu.SemaphoreType.DMA,
            pltpu.SemaphoreType.DMA,
        ],
        compiler_params=pltpu.CompilerParams(dimension_semantics=("arbitrary",), vmem_limit_bytes=VMEM_LIMIT),
        name="combine",
    )(dest_flat, ys, h1, top_w, g2, b2)


def _largest_tile(n, cap):
    t = cap
    while n % t:
        t //= 2
    return t


def kernel(x, meta_tokens, emb_ln_g, emb_ln_b, w_in, conv_w, a_log, dt_bias, gdn_norm_w, attn_sinks, w_br_gdn,
           w_br_swa, w_out, ln1_g, ln1_b, w_router, b_router, w_moe1, b_moe1, w_moe2, b_moe2, ln2_g, ln2_b):
    bsz, seq, d = x.shape
    assert seq % WINDOW == 0 and seq % CHUNK == 0 and d == GDN_VD
    n = bsz * seq
    x2d = x.reshape(n, d)
    row1 = lambda v: v.reshape(1, -1)
    l = 0

    sp = np.cumsum((GDN_QK, GDN_QK, GDN_VD, GDN_VD, GDN_HEADS, GDN_HEADS, SWA_QD, SWA_KVD, SWA_KVD, d, d))
    w = w_in[l]
    w_perm = jnp.concatenate([
        w[:, :sp[3]], w[:, sp[5]:sp[10]], w[:, sp[3]:sp[5]],
        jnp.zeros((d, LANES - 2 * GDN_HEADS), w.dtype)], axis=1).astype(BF16)
    eg, eb = row1(emb_ln_g), row1(emb_ln_b)

    tm = _largest_tile(n, 512)
    qkv_g, z_g, q_s, kv_s, gates, ba = _in_proj(x2d, eg, eb, w_perm, tm)
    m_qkv, _, _, m_kv, _, m_ba = _in_proj(meta_tokens.astype(x.dtype), eg, eb, w_perm, N_META)

    n_pad = CHUNK - N_META
    n_chunks = seq // CHUNK
    conv_w_t = conv_w[l].T
    pad_rows = lambda a: jnp.pad(a, ((n_pad, 0), (0, 0)))
    chunk_t = lambda a, nb: a[:, GDN_HEADS:2 * GDN_HEADS].reshape(nb, CHUNK, GDN_HEADS).transpose(0, 2, 1)
    m_ba_p = pad_rows(m_ba)
    zeros_state = jnp.zeros((GDN_HEADS, GDN_DK, GDN_DV), F32)
    s_meta = _gdn(pad_rows(m_qkv), jnp.zeros((CHUNK, GDN_VD), BF16), m_ba_p, chunk_t(m_ba_p, 1),
                  jnp.zeros((_HIST, m_qkv.shape[1]), BF16), zeros_state, conv_w_t, a_log[l], dt_bias[l],
                  gdn_norm_w[l], 1, 1, n_pad, True)
    o_gdn = _gdn(qkv_g, z_g, ba, chunk_t(ba, bsz * n_chunks), m_qkv[N_META - _HIST:], s_meta, conv_w_t,
                 a_log[l], dt_bias[l], gdn_norm_w[l], bsz, n_chunks, 0, False)

    o_swa = _swa(q_s, kv_s, m_kv, attn_sinks[l], bsz, seq // WINDOW)

    wr = jnp.pad(w_router[l], ((0, 0), (0, LANES - N_EXPERTS)))
    br = jnp.pad(row1(b_router[l]), ((0, 0), (0, LANES - N_EXPERTS)), constant_values=NEG_BIG)
    h1, h1p, top_idx, top_w = _merge(
        x2d, eg, eb, o_gdn, o_swa, gates, w_br_gdn[l].astype(BF16), w_br_swa[l].astype(BF16),
        w_out[l].astype(BF16), row1(ln1_g[l]), row1(ln1_b[l]), wr, br, tm)

    rank, counts = _rank(top_idx, _largest_tile(n, 512))
    counts = counts[0, :N_EXPERTS]
    padded = (counts + MOE_TILE - 1) // MOE_TILE * MOE_TILE
    pad_end = jnp.cumsum(padded)
    pad_start = pad_end - padded
    dest = (pad_start[top_idx] + rank).reshape(-1)
    n_blk = -(-n * TOP_K // MOE_TILE) + N_EXPERTS
    blk_e = jnp.minimum(jnp.searchsorted(pad_end, jnp.arange(n_blk, dtype=jnp.int32) * MOE_TILE, side='right'),
                        N_EXPERTS - 1).astype(jnp.int32)
    n_used = (pad_end[-1:] // MOE_TILE).astype(jnp.int32)

    tg = _largest_tile(n, GATHER_TOKENS)
    xs = _dispatch(dest, h1p, jnp.zeros((n_blk * MOE_TILE, d // 2), jnp.uint32), tg)
    ys = _experts(blk_e, n_used, xs, w_moe1[l].astype(BF16), b_moe1[l], w_moe2[l].astype(BF16), b_moe2[l])
    out = _combine(dest, ys, h1, top_w, row1(ln2_g[l]), row1(ln2_b[l]), tg)
    return out.reshape(bsz, seq, d)
```

```python
import functools

import jax
import jax.numpy as jnp
import numpy as np
from jax import lax
from jax.experimental import pallas as pl
from jax.experimental.pallas import tpu as pltpu

N_META = 16
GDN_HEADS = 8
GDN_DK = 128
GDN_DV = 128
CONV_WIDTH = 4
CHUNK = 64
SWA_Q_HEADS = 16
SWA_KV_HEADS = 4
SWA_HEAD_DIM = 64
SWA_GROUP = SWA_Q_HEADS // SWA_KV_HEADS
WINDOW = 128
N_EXPERTS = 32
TOP_K = 4
SWIGLU_ALPHA = 1.702
SWIGLU_LIMIT = 7.0
LN_EPS = 1e-5
RMS_EPS = 1e-6
DEPTH = 1
DEEPNORM_ALPHA = (2.0 * DEPTH) ** 0.25

GDN_QK = GDN_HEADS * GDN_DK
GDN_VD = GDN_HEADS * GDN_DV
SWA_QD = SWA_Q_HEADS * SWA_HEAD_DIM
SWA_KVD = SWA_KV_HEADS * SWA_HEAD_DIM

LANES = 128
MOE_TILE = 512
GATHER_TOKENS = 256
NEG_BIG = -1e30
VMEM_LIMIT = 56 * 1024 * 1024

F32 = jnp.float32
BF16 = jnp.bfloat16
HIGHEST = lax.Precision.HIGHEST


def _layer_norm(x, g, b):
    mu = jnp.mean(x, -1, keepdims=True)
    xc = x - mu
    var = jnp.mean(xc * xc, -1, keepdims=True)
    return xc * lax.rsqrt(var + LN_EPS) * g + b


def _sigmoid(x):
    return 1.0 / (1.0 + jnp.exp(-x))


def _silu(x):
    return x * _sigmoid(x)


def _softplus(x):
    return jnp.maximum(x, 0.0) + jnp.log(1.0 + jnp.exp(-jnp.abs(x)))


def _dot(a, b, **kw):
    return jnp.dot(a, b, preferred_element_type=F32, **kw)


def _dot_nt(a, b):
    return lax.dot_general(a, b, (((1,), (1,)), ((), ())), preferred_element_type=F32)


def _split_bf16(x):
    hi = x.astype(BF16)
    return hi, (x - hi.astype(F32)).astype(BF16)


def _dot3(a, b):
    return _dot(a[0], b[0]) + (_dot(a[0], b[1]) + _dot(a[1], b[0]))


_C_QKVG = 2 * GDN_QK + GDN_VD
_C_Z = _C_QKVG + GDN_VD
_C_QS = _C_Z + SWA_QD
_C_KVS = _C_QS + 2 * SWA_KVD
_C_GATES = _C_KVS + 2 * GDN_VD
_C_BA = _C_GATES + LANES


def _inproj_kernel(x_ref, g_ref, b_ref, w_ref, qkvg_ref, z_ref, qs_ref, kvs_ref, gates_ref, ba_ref):
    h = _layer_norm(x_ref[...], g_ref[...], b_ref[...]).astype(BF16)

    def proj(out_ref, lo, hi, step=1024):
        for c in range(lo, hi, step):
            ce = min(c + step, hi)
            out_ref[:, c - lo:ce - lo] = _dot(h, w_ref[:, c:ce]).astype(out_ref.dtype)

    proj(qkvg_ref, 0, _C_QKVG)
    proj(z_ref, _C_QKVG, _C_Z)
    proj(qs_ref, _C_Z, _C_QS)
    proj(kvs_ref, _C_QS, _C_KVS)
    proj(gates_ref, _C_KVS, _C_GATES)
    proj(ba_ref, _C_GATES, _C_BA)


def _in_proj(x2d, ln_g, ln_b, w_perm, tm):
    n, d = x2d.shape
    widths = (_C_QKVG, GDN_VD, SWA_QD, 2 * SWA_KVD, _C_GATES - _C_KVS, LANES)
    dtypes = (BF16, BF16, BF16, BF16, BF16, F32)
    row = lambda i: (i, 0)
    const = lambda i: (0, 0)
    return pl.pallas_call(
        _inproj_kernel,
        grid=(n // tm,),
        in_specs=[
            pl.BlockSpec((tm, d), row),
            pl.BlockSpec((1, d), const),
            pl.BlockSpec((1, d), const),
            pl.BlockSpec((d, _C_BA), const, pipeline_mode=pl.Buffered(1)),
        ],
        out_specs=[pl.BlockSpec((tm, w), row) for w in widths],
        out_shape=[jax.ShapeDtypeStruct((n, w), dt) for w, dt in zip(widths, dtypes)],
        compiler_params=pltpu.CompilerParams(dimension_semantics=("parallel",), vmem_limit_bytes=VMEM_LIMIT),
        name="in_proj",
    )(x2d, ln_g, ln_b, w_perm)


_HIST = 8


def _gdn_kernel(qkv_ref, z_ref, ba_ref, at_ref, hist_ref, s0_ref, convw_ref, alog_r_ref, dtb_r_ref,
                alog_c_ref, dtb_c_ref, normw_ref, *rest, n_pad, emit_state):
    if emit_state:
        s_out_ref, s_ref, win_ref = rest
    else:
        o_ref, s_ref, win_ref = rest
    n = pl.program_id(1)
    c = CHUNK

    @pl.when(n == 0)
    def _():
        s_ref[...] = s0_ref[...]
        win_ref[0:_HIST, :] = hist_ref[...].astype(F32)

    win_ref[_HIST:_HIST + c, :] = qkv_ref[...].astype(F32)
    acc = win_ref[_HIST:_HIST + c, :] * convw_ref[CONV_WIDTH - 1:CONV_WIDTH, :]
    for i in range(CONV_WIDTH - 1):
        sh = CONV_WIDTH - 1 - i
        acc = acc + win_ref[_HIST - sh:_HIST - sh + c, :] * convw_ref[i:i + 1, :]
    win_ref[0:_HIST, :] = win_ref[c:c + _HIST, :]
    qkv = _silu(acc)

    row = lax.broadcasted_iota(jnp.int32, (c, c), 0)
    col = lax.broadcasted_iota(jnp.int32, (c, c), 1)
    incl = row >= col
    strict = row > col
    tri_incl = incl.astype(F32)
    tri_upper = (row <= col).astype(F32)
    eye = (row == col).astype(F32)

    beta = _sigmoid(ba_ref[:, 0:GDN_HEADS])
    log_g_c = -jnp.exp(alog_r_ref[...]) * _softplus(ba_ref[:, GDN_HEADS:2 * GDN_HEADS] + dtb_r_ref[...])
    log_g_r = -jnp.exp(alog_c_ref[...]) * _softplus(at_ref[...] + dtb_c_ref[...])
    if n_pad:
        valid_c = lax.broadcasted_iota(jnp.int32, (c, GDN_HEADS), 0) >= n_pad
        valid_r = lax.broadcasted_iota(jnp.int32, (GDN_HEADS, c), 1) >= n_pad
        beta = jnp.where(valid_c, beta, 0.0)
        log_g_c = jnp.where(valid_c, log_g_c, 0.0)
        log_g_r = jnp.where(valid_r, log_g_r, 0.0)
    gam_c = _dot(tri_incl, log_g_c, precision=HIGHEST)
    gam_r = _dot(log_g_r, tri_upper, precision=HIGHEST)

    heads = range(GDN_HEADS)
    kn, kn_b, decay, a_mat, rhs, qk_b, wq_b, kdt_b, g_tot = [], [], [], [], [], [], [], [], []
    qn_l, eg_l = [], []
    for h in heads:
        qh = qkv[:, h * GDN_DK:(h + 1) * GDN_DK]
        kh = qkv[:, GDN_QK + h * GDN_DK:GDN_QK + (h + 1) * GDN_DK]
        vh = qkv[:, 2 * GDN_QK + h * GDN_DV:2 * GDN_QK + (h + 1) * GDN_DV]
        qn = qh * lax.rsqrt(jnp.sum(qh * qh, -1, keepdims=True) + RMS_EPS) * (GDN_DK ** -0.5)
        k_n = kh * lax.rsqrt(jnp.sum(kh * kh, -1, keepdims=True) + RMS_EPS)
        b_c = beta[:, h:h + 1]
        g_c = gam_c[:, h:h + 1]
        g_r = gam_r[h:h + 1, :]
        g_last = g_c[c - 1:c, :]
        e_g = jnp.exp(g_c)
        kn.append(k_n)
        kn_b.append(k_n.astype(BF16))
        qn_l.append(qn)
        eg_l.append(e_g)
        decay.append(jnp.where(incl, jnp.exp(jnp.where(incl, g_c - g_r, 0.0)), 0.0))
        rhs.append(_split_bf16(jnp.concatenate([vh * b_c, k_n * (b_c * e_g)], axis=-1)))
        kdt_b.append((k_n * jnp.exp(g_last - g_c)).T.astype(BF16))
        g_tot.append(jnp.exp(g_last))
    for h in heads:
        kk = _dot_nt(kn_b[h], kn_b[h])
        a_mat.append(jnp.where(strict, kk * decay[h], 0.0) * beta[:, h:h + 1])
        qk_b.append((_dot_nt(qn_l[h].astype(BF16), kn_b[h]) * decay[h]).astype(BF16))

    t_inv = [eye - a for a in a_mat]
    a_split = [_split_bf16(a) for a in a_mat]
    for _ in range(5):
        a_split = [_split_bf16(_dot3(s, s)) for s in a_split]
        t_inv = [t + _dot3(_split_bf16(t), s) for t, s in zip(t_inv, a_split)]
    sol = [_dot3(_split_bf16(t), r) for t, r in zip(t_inv, rhs)]

    state = [s_ref[h] for h in heads]
    for h in heads:
        wq_b.append(jnp.concatenate([sol[h][:, GDN_DV:], qn_l[h] * eg_l[h]], axis=0).astype(BF16))
    p1 = [_dot(wq_b[h], state[h].astype(BF16)) for h in heads]
    u_b = [(sol[h][:, :GDN_DV] - p1[h][:c]).astype(BF16) for h in heads]
    for h in heads:
        s_ref[h] = state[h] * g_tot[h] + _dot(kdt_b[h], u_b[h])
    if not emit_state:
        for h in heads:
            o = p1[h][c:] + _dot(qk_b[h], u_b[h])
            o = o * lax.rsqrt(jnp.mean(o * o, -1, keepdims=True) + RMS_EPS) * normw_ref[...]
            zh = z_ref[:, h * GDN_DV:(h + 1) * GDN_DV].astype(F32)
            o_ref[:, h * GDN_DV:(h + 1) * GDN_DV] = (o * _silu(zh)).astype(o_ref.dtype)

    if emit_state:
        s_out_ref[...] = s_ref[...]


def _gdn(qkv, z, ba, a_t, hist, s0, conv_w_t, a_log, dt_bias, norm_w, bsz, n_chunks, n_pad, emit_state):
    c = CHUNK
    rows = lambda b, n: (b * n_chunks + n, 0)
    c2 = lambda b, n: (0, 0)
    c3 = lambda b, n: (0, 0, 0)
    in_specs = [
        pl.BlockSpec((c, qkv.shape[1]), rows),
        pl.BlockSpec((c, z.shape[1]), rows),
        pl.BlockSpec((c, ba.shape[1]), rows),
        pl.BlockSpec((None, GDN_HEADS, c), lambda b, n: (b * n_chunks + n, 0, 0)),
        pl.BlockSpec(hist.shape, c2),
        pl.BlockSpec(s0.shape, c3),
        pl.BlockSpec(conv_w_t.shape, c2),
        pl.BlockSpec((1, GDN_HEADS), c2),
        pl.BlockSpec((1, GDN_HEADS), c2),
        pl.BlockSpec((GDN_HEADS, 1), c2),
        pl.BlockSpec((GDN_HEADS, 1), c2),
        pl.BlockSpec((1, GDN_DV), c2),
    ]
    if emit_state:
        out_specs = pl.BlockSpec(s0.shape, c3)
        out_shape = jax.ShapeDtypeStruct(s0.shape, F32)
    else:
        out_specs = pl.BlockSpec((c, GDN_VD), rows)
        out_shape = jax.ShapeDtypeStruct((qkv.shape[0], GDN_VD), BF16)
    return pl.pallas_call(
        functools.partial(_gdn_kernel, n_pad=n_pad, emit_state=emit_state),
        grid=(bsz, n_chunks),
        in_specs=in_specs,
        out_specs=out_specs,
        out_shape=out_shape,
        scratch_shapes=[
            pltpu.VMEM((GDN_HEADS, GDN_DK, GDN_DV), F32),
            pltpu.VMEM((_HIST + c, qkv.shape[1]), F32),
        ],
        compiler_params=pltpu.CompilerParams(dimension_semantics=("parallel", "arbitrary"),
                                             vmem_limit_bytes=VMEM_LIMIT),
        name="gdn_meta" if emit_state else "gdn",
    )(qkv, z, ba, a_t, hist, s0, conv_w_t, a_log.reshape(1, -1), dt_bias.reshape(1, -1),
      a_log.reshape(-1, 1), dt_bias.reshape(-1, 1), norm_w.reshape(1, -1))


def _swa_kernel(q_ref, kvc_ref, kvp_ref, kvm_ref, sink_ref, o_ref):
    i = pl.program_id(1)
    blk = WINDOW
    hd = SWA_HEAD_DIM
    rows = SWA_GROUP * blk
    t = lax.broadcasted_iota(jnp.int32, (rows, blk), 0) & (blk - 1)
    s = lax.broadcasted_iota(jnp.int32, (rows, blk), 1)
    d_cur = (t - s).astype(F32)
    m_cur = t >= s
    m_prev = jnp.logical_and(s > t, i > 0)
    tm_ = lax.broadcasted_iota(jnp.int32, (rows, N_META), 0) & (blk - 1)
    sm_ = lax.broadcasted_iota(jnp.int32, (rows, N_META), 1)
    d_meta = (tm_ - sm_ + N_META + i * blk).astype(F32)
    ones = jnp.ones((blk, hd), BF16)
    ones_meta = jnp.ones((N_META, hd), BF16)
    groups = range(SWA_KV_HEADS)

    def per_row(vals):
        return jnp.concatenate([jnp.broadcast_to(v, (blk, 1)) for v in vals], axis=0)

    sc, sinks = [], []
    for g in groups:
        hqs = [g * SWA_GROUP + j for j in range(SWA_GROUP)]
        q_g = jnp.concatenate([q_ref[:, hq * hd:(hq + 1) * hd] for hq in hqs], axis=0) * (hd ** -0.5)
        slope = per_row([jnp.full((1, 1), 2.0 ** (-8.0 * (hq + 1) / SWA_Q_HEADS), F32) for hq in hqs])
        bias_cur = slope * d_cur
        sc_cur = jnp.where(m_cur, _dot_nt(q_g, kvc_ref[:, g * hd:(g + 1) * hd]) - bias_cur, NEG_BIG)
        sc_prev = jnp.where(m_prev, _dot_nt(q_g, kvp_ref[:, g * hd:(g + 1) * hd]) - (bias_cur + slope * blk),
                            NEG_BIG)
        sc_meta = _dot_nt(q_g, kvm_ref[:, g * hd:(g + 1) * hd]) - slope * d_meta
        sc.append((sc_cur, sc_prev, sc_meta))
        sinks.append(per_row([sink_ref[:, hq:hq + 1] for hq in hqs]))
    e, e_sink = [], []
    for g in groups:
        sc_cur, sc_prev, sc_meta = sc[g]
        m = jnp.maximum(jnp.maximum(jnp.max(jnp.maximum(sc_cur, sc_prev), -1, keepdims=True),
                                    jnp.max(sc_meta, -1, keepdims=True)), sinks[g])
        e.append(tuple(jnp.exp(x - m).astype(BF16) for x in sc[g]))
        e_sink.append(jnp.exp(sinks[g] - m))
    for g in groups:
        v_of = lambda ref, pad: jnp.concatenate([ref[:, SWA_KVD + g * hd:SWA_KVD + (g + 1) * hd], pad], axis=-1)
        acc = (_dot(e[g][0], v_of(kvc_ref, ones)) + _dot(e[g][1], v_of(kvp_ref, ones))
               + _dot(e[g][2], v_of(kvm_ref, ones_meta)))
        o = acc[:, :hd] / (acc[:, hd:hd + 1] + e_sink[g])
        for j in range(SWA_GROUP):
            hq = g * SWA_GROUP + j
            o_ref[:, hq * hd:(hq + 1) * hd] = o[j * blk:(j + 1) * blk].astype(o_ref.dtype)


def _swa(q, kv, kv_meta, sinks, bsz, nq):
    blk = WINDOW
    return pl.pallas_call(
        _swa_kernel,
        grid=(bsz, nq),
        in_specs=[
            pl.BlockSpec((blk, SWA_QD), lambda b, i: (b * nq + i, 0)),
            pl.BlockSpec((blk, 2 * SWA_KVD), lambda b, i: (b * nq + i, 0)),
            pl.BlockSpec((blk, 2 * SWA_KVD), lambda b, i: (b * nq + jnp.maximum(i - 1, 0), 0)),
            pl.BlockSpec((N_META, 2 * SWA_KVD), lambda b, i: (0, 0)),
            pl.BlockSpec((1, SWA_Q_HEADS), lambda b, i: (0, 0)),
        ],
        out_specs=pl.BlockSpec((blk, SWA_QD), lambda b, i: (b * nq + i, 0)),
        out_shape=jax.ShapeDtypeStruct((q.shape[0], SWA_QD), BF16),
        compiler_params=pltpu.CompilerParams(dimension_semantics=("parallel", "parallel"),
                                             vmem_limit_bytes=VMEM_LIMIT),
        name="swa",
    )(q, kv, kv, kv_meta, sinks.reshape(1, -1))


def _merge_kernel(x_ref, eg_ref, eb_ref, og_ref, os_ref, gates_ref, wg_ref, ws_ref, wo_ref, g1_ref, b1_ref,
                  wr_ref, br_ref, h1_ref, idx_ref, tw_ref):
    d = x_ref.shape[1]
    h0 = _layer_norm(x_ref[...], eg_ref[...], eb_ref[...])
    y_g = _dot(og_ref[...], wg_ref[...])
    y_s = _dot(os_ref[...], ws_ref[...])
    mix = _sigmoid(gates_ref[:, :d].astype(F32)) * y_g + _sigmoid(gates_ref[:, d:].astype(F32)) * y_s
    mixed = _dot(mix.astype(BF16), wo_ref[...])
    h1 = _layer_norm(DEEPNORM_ALPHA * h0 + mixed, g1_ref[...], b1_ref[...])
    h1_ref[...] = h1

    logits = _dot(h1, wr_ref[...], precision=HIGHEST) + br_ref[...]
    lane = lax.broadcasted_iota(jnp.int32, logits.shape, 1)
    vals, idxs = [], []
    for _ in range(TOP_K):
        m = jnp.max(logits, -1, keepdims=True)
        sel = jnp.min(jnp.where(logits == m, lane, LANES), -1, keepdims=True)
        vals.append(m)
        idxs.append(sel)
        logits = jnp.where(lane == sel, NEG_BIG, logits)
    e = [jnp.exp(v - vals[0]) for v in vals]
    tot = e[0] + e[1] + e[2] + e[3]
    for k in range(TOP_K):
        idx_ref[:, k:k + 1] = idxs[k]
        tw_ref[:, k:k + 1] = e[k] / tot


def _merge(x2d, eg, eb, o_gdn, o_swa, gates, wg, ws, wo, g1, b1, wr, br, tm):
    n, d = x2d.shape
    row = lambda i: (i, 0)
    const = lambda i: (0, 0)
    full = lambda a: pl.BlockSpec(a.shape, const)
    return pl.pallas_call(
        _merge_kernel,
        grid=(n // tm,),
        in_specs=[
            pl.BlockSpec((tm, d), row), full(eg), full(eb),
            pl.BlockSpec((tm, GDN_VD), row), pl.BlockSpec((tm, SWA_QD), row), pl.BlockSpec((tm, 2 * d), row),
            full(wg), full(ws), full(wo), full(g1), full(b1), full(wr), full(br),
        ],
        out_specs=[
            pl.BlockSpec((tm, d), row), pl.BlockSpec((tm, TOP_K), row), pl.BlockSpec((tm, TOP_K), row),
        ],
        out_shape=[
            jax.ShapeDtypeStruct((n, d), F32),
            jax.ShapeDtypeStruct((n, TOP_K), jnp.int32), jax.ShapeDtypeStruct((n, TOP_K), F32),
        ],
        compiler_params=pltpu.CompilerParams(dimension_semantics=("parallel",), vmem_limit_bytes=VMEM_LIMIT),
        name="merge",
    )(x2d, eg, eb, o_gdn, o_swa, gates, wg, ws, wo, g1, b1, wr, br)


def _rank_kernel(idx_ref, rank_ref, counts_ref, run_ref):
    t = idx_ref.shape[0]

    @pl.when(pl.program_id(0) == 0)
    def _():
        run_ref[...] = jnp.zeros_like(run_ref)

    lane = lax.broadcasted_iota(jnp.int32, (t, LANES), 1)
    hits = [idx_ref[:, k:k + 1] == lane for k in range(TOP_K)]
    onehot = hits[0].astype(F32)
    for k in range(1, TOP_K):
        onehot = onehot + hits[k].astype(F32)
    r = lax.broadcasted_iota(jnp.int32, (t, t), 0)
    c = lax.broadcasted_iota(jnp.int32, (t, t), 1)
    before = (r > c).astype(BF16)
    base = run_ref[...] + _dot(before, onehot.astype(BF16))
    for k in range(TOP_K):
        rank_ref[:, k:k + 1] = jnp.sum(jnp.where(hits[k], base, 0.0), -1, keepdims=True).astype(jnp.int32)
    run_ref[...] = run_ref[...] + jnp.sum(onehot, 0, keepdims=True)
    counts_ref[...] = run_ref[...].astype(jnp.int32)


def _rank(top_idx, t):
    n = top_idx.shape[0]
    return pl.pallas_call(
        _rank_kernel,
        grid=(n // t,),
        in_specs=[pl.BlockSpec((t, TOP_K), lambda i: (i, 0))],
        out_specs=[pl.BlockSpec((t, TOP_K), lambda i: (i, 0)), pl.BlockSpec((1, LANES), lambda i: (0, 0))],
        out_shape=[jax.ShapeDtypeStruct((n, TOP_K), jnp.int32), jax.ShapeDtypeStruct((1, LANES), jnp.int32)],
        scratch_shapes=[pltpu.VMEM((1, LANES), F32)],
        compiler_params=pltpu.CompilerParams(dimension_semantics=("arbitrary",)),
        name="rank",
    )(top_idx)


_ROW_GROUP = 8


def _row_copy(src, dst, sem):
    return pltpu.make_async_copy(src, dst, sem)


def _dispatch_kernel(pad_end_ref, dest_hbm, h_ref, xs_hbm, idx_smem, zeros_ref, sem_idx, sem_rows, sem_zero):
    i = pl.program_id(0)
    t = h_ref.shape[0]
    n_idx = t * TOP_K

    @pl.when(i == 0)
    def _():
        zeros_ref[...] = jnp.zeros_like(zeros_ref)
        for e in range(N_EXPERTS):
            end = pad_end_ref[e]
            start = pad_end_ref[e - 1] if e else 0

            @pl.when(end > start)
            def _():
                tile0 = pl.multiple_of(end - MOE_TILE, MOE_TILE)
                fill = _row_copy(zeros_ref, xs_hbm.at[pl.ds(tile0, MOE_TILE)], sem_zero)
                fill.start()
                fill.wait()

    idx_cp = _row_copy(dest_hbm.at[pl.ds(pl.multiple_of(i * n_idx, n_idx), n_idx)], idx_smem, sem_idx)
    idx_cp.start()
    idx_cp.wait()

    def issue(grp, carry):
        row0 = pl.multiple_of(grp * _ROW_GROUP, _ROW_GROUP)
        for r in range(_ROW_GROUP):
            for k in range(TOP_K):
                dst_row = idx_smem[(row0 + r) * TOP_K + k]
                _row_copy(h_ref.at[pl.ds(row0 + r, 1)], xs_hbm.at[pl.ds(dst_row, 1)], sem_rows).start()
        return carry

    lax.fori_loop(0, t // _ROW_GROUP, issue, 0)
    _row_copy(xs_hbm.at[pl.ds(0, n_idx)], xs_hbm.at[pl.ds(0, n_idx)], sem_rows).wait()


def _dispatch(pad_end, dest_flat, h1, n_rows, t):
    n, d = h1.shape
    grid_spec = pltpu.PrefetchScalarGridSpec(
        num_scalar_prefetch=1,
        grid=(n // t,),
        in_specs=[
            pl.BlockSpec(memory_space=pl.ANY),
            pl.BlockSpec((t, d), lambda i, pe: (i, 0)),
        ],
        out_specs=pl.BlockSpec(memory_space=pl.ANY),
        scratch_shapes=[
            pltpu.SMEM((t * TOP_K,), jnp.int32),
            pltpu.VMEM((MOE_TILE, d), h1.dtype),
            pltpu.SemaphoreType.DMA, pltpu.SemaphoreType.DMA, pltpu.SemaphoreType.DMA,
        ],
    )
    return pl.pallas_call(
        _dispatch_kernel,
        grid_spec=grid_spec,
        out_shape=jax.ShapeDtypeStruct((n_rows, d), h1.dtype),
        compiler_params=pltpu.CompilerParams(dimension_semantics=("arbitrary",), vmem_limit_bytes=VMEM_LIMIT),
        name="dispatch",
    )(pad_end, dest_flat, h1)


def _expert_kernel(blk_e_ref, n_used_ref, xs_ref, w1_ref, b1_ref, w2_ref, b2_ref, ys_ref):
    del blk_e_ref
    i = pl.program_id(0)
    d_ff = w2_ref.shape[0]

    @pl.when(i < n_used_ref[0])
    def _():
        hdn = _dot(xs_ref[...].astype(BF16), w1_ref[...]) + b1_ref[...]
        gate = jnp.minimum(hdn[:, :d_ff], SWIGLU_LIMIT)
        up = jnp.clip(hdn[:, d_ff:], -SWIGLU_LIMIT, SWIGLU_LIMIT)
        act = gate * _sigmoid(SWIGLU_ALPHA * gate) * (up + 1.0)
        ys_ref[...] = _dot(act.astype(BF16), w2_ref[...]) + b2_ref[...]

    @pl.when(i >= n_used_ref[0])
    def _():
        ys_ref[...] = jnp.zeros_like(ys_ref)


def _experts(blk_e, n_used, xs, w1, b1, w2, b2):
    n_rows, w = xs.shape
    n_e, d, d_ff2 = w1.shape
    d_ff = w2.shape[1]
    grid_spec = pltpu.PrefetchScalarGridSpec(
        num_scalar_prefetch=2,
        grid=(n_rows // MOE_TILE,),
        in_specs=[
            pl.BlockSpec((MOE_TILE, w), lambda i, be, nu: (jnp.minimum(i, nu[0] - 1), 0)),
            pl.BlockSpec((None, d, d_ff2), lambda i, be, nu: (be[i], 0, 0)),
            pl.BlockSpec((None, 1, d_ff2), lambda i, be, nu: (be[i], 0, 0)),
            pl.BlockSpec((None, d_ff, d), lambda i, be, nu: (be[i], 0, 0)),
            pl.BlockSpec((None, 1, d), lambda i, be, nu: (be[i], 0, 0)),
        ],
        out_specs=pl.BlockSpec((MOE_TILE, w), lambda i, be, nu: (i, 0)),
    )
    return pl.pallas_call(
        _expert_kernel,
        grid_spec=grid_spec,
        out_shape=jax.ShapeDtypeStruct((n_rows, w), F32),
        compiler_params=pltpu.CompilerParams(dimension_semantics=("arbitrary",), vmem_limit_bytes=VMEM_LIMIT),
        name="experts",
    )(blk_e, n_used, xs, w1, b1.reshape(n_e, 1, d_ff2), w2, b2.reshape(n_e, 1, d))


def _combine_kernel(dest_hbm, ys_hbm, h1_ref, tw_ref, g_ref, b_ref, out_ref, idx_smem, buf, sem_idx, sem_rows):
    i = pl.program_id(0)
    t = h1_ref.shape[0]
    n_idx = t * TOP_K
    idx_cp = _row_copy(dest_hbm.at[pl.ds(pl.multiple_of(i * n_idx, n_idx), n_idx)], idx_smem, sem_idx)
    idx_cp.start()
    idx_cp.wait()

    def issue(grp, carry):
        row0 = pl.multiple_of(grp * _ROW_GROUP, _ROW_GROUP)
        for r in range(_ROW_GROUP):
            for k in range(TOP_K):
                src_row = idx_smem[(row0 + r) * TOP_K + k]
                _row_copy(ys_hbm.at[pl.ds(src_row, 1)], buf.at[k, pl.ds(row0 + r, 1)], sem_rows).start()
        return carry

    lax.fori_loop(0, t // _ROW_GROUP, issue, 0)
    _row_copy(buf, buf, sem_rows).wait()

    ffn = None
    for k in range(TOP_K):
        term = buf[k] * tw_ref[:, k:k + 1]
        ffn = term if ffn is None else ffn + term
    out_ref[...] = _layer_norm(DEEPNORM_ALPHA * h1_ref[...] + ffn, g_ref[...], b_ref[...])


def _combine(dest_flat, ys, h1, top_w, g2, b2, t):
    n, d = h1.shape
    w = ys.shape[1]
    return pl.pallas_call(
        _combine_kernel,
        grid=(n // t,),
        in_specs=[
            pl.BlockSpec(memory_space=pl.ANY),
            pl.BlockSpec(memory_space=pl.ANY),
            pl.BlockSpec((t, d), lambda i: (i, 0)),
            pl.BlockSpec((t, TOP_K), lambda i: (i, 0)),
            pl.BlockSpec((1, d), lambda i: (0, 0)),
            pl.BlockSpec((1, d), lambda i: (0, 0)),
        ],
        out_specs=pl.BlockSpec((t, d), lambda i: (i, 0)),
        out_shape=jax.ShapeDtypeStruct((n, d), F32),
        scratch_shapes=[
            pltpu.SMEM((t * TOP_K,), jnp.int32),
            pltpu.VMEM((TOP_K, t, w), ys.dtype),
            pltpu.SemaphoreType.DMA,
            pltpu.SemaphoreType.DMA,
        ],
        compiler_params=pltpu.CompilerParams(dimension_semantics=("arbitrary",), vmem_limit_bytes=VMEM_LIMIT),
        name="combine",
    )(dest_flat, ys, h1, top_w, g2, b2)


def _largest_tile(n, cap):
    t = cap
    while n % t:
        t //= 2
    return t


def kernel(x, meta_tokens, emb_ln_g, emb_ln_b, w_in, conv_w, a_log, dt_bias, gdn_norm_w, attn_sinks, w_br_gdn,
           w_br_swa, w_out, ln1_g, ln1_b, w_router, b_router, w_moe1, b_moe1, w_moe2, b_moe2, ln2_g, ln2_b):
    bsz, seq, d = x.shape
    assert seq % WINDOW == 0 and seq % CHUNK == 0 and d == GDN_VD
    n = bsz * seq
    x2d = x.reshape(n, d)
    row1 = lambda v: v.reshape(1, -1)
    l = 0

    sp = np.cumsum((GDN_QK, GDN_QK, GDN_VD, GDN_VD, GDN_HEADS, GDN_HEADS, SWA_QD, SWA_KVD, SWA_KVD, d, d))
    w = w_in[l]
    w_perm = jnp.concatenate([
        w[:, :sp[3]], w[:, sp[5]:sp[10]], w[:, sp[3]:sp[5]],
        jnp.zeros((d, LANES - 2 * GDN_HEADS), w.dtype)], axis=1).astype(BF16)
    eg, eb = row1(emb_ln_g), row1(emb_ln_b)

    tm = _largest_tile(n, 512)
    qkv_g, z_g, q_s, kv_s, gates, ba = _in_proj(x2d, eg, eb, w_perm, tm)
    m_qkv, _, _, m_kv, _, m_ba = _in_proj(meta_tokens.astype(x.dtype), eg, eb, w_perm, N_META)

    n_pad = CHUNK - N_META
    n_chunks = seq // CHUNK
    conv_w_t = conv_w[l].T
    pad_rows = lambda a: jnp.pad(a, ((n_pad, 0), (0, 0)))
    chunk_t = lambda a, nb: a[:, GDN_HEADS:2 * GDN_HEADS].reshape(nb, CHUNK, GDN_HEADS).transpose(0, 2, 1)
    m_ba_p = pad_rows(m_ba)
    zeros_state = jnp.zeros((GDN_HEADS, GDN_DK, GDN_DV), F32)
    s_meta = _gdn(pad_rows(m_qkv), jnp.zeros((CHUNK, GDN_VD), BF16), m_ba_p, chunk_t(m_ba_p, 1),
                  jnp.zeros((_HIST, m_qkv.shape[1]), BF16), zeros_state, conv_w_t, a_log[l], dt_bias[l],
                  gdn_norm_w[l], 1, 1, n_pad, True)
    o_gdn = _gdn(qkv_g, z_g, ba, chunk_t(ba, bsz * n_chunks), m_qkv[N_META - _HIST:], s_meta, conv_w_t,
                 a_log[l], dt_bias[l], gdn_norm_w[l], bsz, n_chunks, 0, False)

    o_swa = _swa(q_s, kv_s, m_kv, attn_sinks[l], bsz, seq // WINDOW)

    wr = jnp.pad(w_router[l], ((0, 0), (0, LANES - N_EXPERTS)))
    br = jnp.pad(row1(b_router[l]), ((0, 0), (0, LANES - N_EXPERTS)), constant_values=NEG_BIG)
    h1, top_idx, top_w = _merge(
        x2d, eg, eb, o_gdn, o_swa, gates, w_br_gdn[l].astype(BF16), w_br_swa[l].astype(BF16),
        w_out[l].astype(BF16), row1(ln1_g[l]), row1(ln1_b[l]), wr, br, tm)

    rank, counts = _rank(top_idx, _largest_tile(n, 512))
    counts = counts[0, :N_EXPERTS]
    padded = (counts + MOE_TILE - 1) // MOE_TILE * MOE_TILE
    pad_end = jnp.cumsum(padded)
    pad_start = pad_end - padded
    dest = (pad_start[top_idx] + rank).reshape(-1)
    n_blk = -(-n * TOP_K // MOE_TILE) + N_EXPERTS
    blk_row0 = jnp.arange(n_blk, dtype=jnp.int32) * MOE_TILE
    blk_e = jnp.minimum(jnp.sum((pad_end[None, :] <= blk_row0[:, None]).astype(jnp.int32), axis=1), N_EXPERTS - 1)
    n_used = (pad_end[-1:] // MOE_TILE).astype(jnp.int32)

    tg = _largest_tile(n, GATHER_TOKENS)
    xs = _dispatch(pad_end.astype(jnp.int32), dest, h1, n_blk * MOE_TILE, tg)
    ys = _experts(blk_e, n_used, xs, w_moe1[l].astype(BF16), b_moe1[l], w_moe2[l].astype(BF16), b_moe2[l])
    out = _combine(dest, ys, h1, top_w, row1(ln2_g[l]), row1(ln2_b[l]), tg)
    return out.reshape(bsz, seq, d)
```

```python
import functools

import jax
import jax.numpy as jnp
import numpy as np
from jax import lax
from jax.experimental import pallas as pl
from jax.experimental.pallas import tpu as pltpu

N_META = 16
GDN_HEADS = 8
GDN_DK = 128
GDN_DV = 128
CONV_WIDTH = 4
CHUNK = 64
SWA_Q_HEADS = 16
SWA_KV_HEADS = 4
SWA_HEAD_DIM = 64
SWA_GROUP = SWA_Q_HEADS // SWA_KV_HEADS
WINDOW = 128
N_EXPERTS = 32
TOP_K = 4
SWIGLU_ALPHA = 1.702
SWIGLU_LIMIT = 7.0
LN_EPS = 1e-5
RMS_EPS = 1e-6
DEPTH = 1
DEEPNORM_ALPHA = (2.0 * DEPTH) ** 0.25

GDN_QK = GDN_HEADS * GDN_DK
GDN_VD = GDN_HEADS * GDN_DV
SWA_QD = SWA_Q_HEADS * SWA_HEAD_DIM
SWA_KVD = SWA_KV_HEADS * SWA_HEAD_DIM

LANES = 128
MOE_TILE = 512
GATHER_TOKENS = 256
NEG_BIG = -1e30
VMEM_LIMIT = 56 * 1024 * 1024

F32 = jnp.float32
BF16 = jnp.bfloat16
HIGHEST = lax.Precision.HIGHEST


def _layer_norm(x, g, b):
    mu = jnp.mean(x, -1, keepdims=True)
    xc = x - mu
    var = jnp.mean(xc * xc, -1, keepdims=True)
    return xc * lax.rsqrt(var + LN_EPS) * g + b


def _sigmoid(x):
    return 1.0 / (1.0 + jnp.exp(-x))


def _silu(x):
    return x * _sigmoid(x)


def _softplus(x):
    return jnp.maximum(x, 0.0) + jnp.log(1.0 + jnp.exp(-jnp.abs(x)))


def _dot(a, b, **kw):
    return jnp.dot(a, b, preferred_element_type=F32, **kw)


def _dot_nt(a, b):
    return lax.dot_general(a, b, (((1,), (1,)), ((), ())), preferred_element_type=F32)


_C_QKVG = 2 * GDN_QK + GDN_VD
_C_Z = _C_QKVG + GDN_VD
_C_QS = _C_Z + SWA_QD
_C_KVS = _C_QS + 2 * SWA_KVD
_C_GATES = _C_KVS + 2 * GDN_VD
_C_BA = _C_GATES + LANES


def _inproj_kernel(x_ref, g_ref, b_ref, w_ref, qkvg_ref, z_ref, qs_ref, kvs_ref, gates_ref, ba_ref):
    h = _layer_norm(x_ref[...], g_ref[...], b_ref[...]).astype(BF16)

    def proj(out_ref, lo, hi, step=1024):
        for c in range(lo, hi, step):
            ce = min(c + step, hi)
            out_ref[:, c - lo:ce - lo] = _dot(h, w_ref[:, c:ce]).astype(out_ref.dtype)

    proj(qkvg_ref, 0, _C_QKVG)
    proj(z_ref, _C_QKVG, _C_Z)
    proj(qs_ref, _C_Z, _C_QS)
    proj(kvs_ref, _C_QS, _C_KVS)
    proj(gates_ref, _C_KVS, _C_GATES)
    proj(ba_ref, _C_GATES, _C_BA)


def _in_proj(x2d, ln_g, ln_b, w_perm, tm):
    n, d = x2d.shape
    widths = (_C_QKVG, GDN_VD, SWA_QD, 2 * SWA_KVD, _C_GATES - _C_KVS, LANES)
    dtypes = (BF16, BF16, BF16, BF16, BF16, F32)
    row = lambda i: (i, 0)
    const = lambda i: (0, 0)
    return pl.pallas_call(
        _inproj_kernel,
        grid=(n // tm,),
        in_specs=[
            pl.BlockSpec((tm, d), row),
            pl.BlockSpec((1, d), const),
            pl.BlockSpec((1, d), const),
            pl.BlockSpec((d, _C_BA), const, pipeline_mode=pl.Buffered(1)),
        ],
        out_specs=[pl.BlockSpec((tm, w), row) for w in widths],
        out_shape=[jax.ShapeDtypeStruct((n, w), dt) for w, dt in zip(widths, dtypes)],
        compiler_params=pltpu.CompilerParams(dimension_semantics=("parallel",), vmem_limit_bytes=VMEM_LIMIT),
        name="in_proj",
    )(x2d, ln_g, ln_b, w_perm)


_HIST = 8


def _gdn_kernel(qkv_ref, z_ref, ba_ref, at_ref, hist_ref, s0_ref, convw_ref, alog_r_ref, dtb_r_ref,
                alog_c_ref, dtb_c_ref, normw_ref, *rest, n_pad, emit_state):
    if emit_state:
        s_out_ref, s_ref, win_ref = rest
    else:
        o_ref, s_ref, win_ref = rest
    n = pl.program_id(1)
    c = CHUNK

    @pl.when(n == 0)
    def _():
        s_ref[...] = s0_ref[...]
        win_ref[0:_HIST, :] = hist_ref[...].astype(F32)

    win_ref[_HIST:_HIST + c, :] = qkv_ref[...].astype(F32)
    acc = win_ref[_HIST:_HIST + c, :] * convw_ref[CONV_WIDTH - 1:CONV_WIDTH, :]
    for i in range(CONV_WIDTH - 1):
        sh = CONV_WIDTH - 1 - i
        acc = acc + win_ref[_HIST - sh:_HIST - sh + c, :] * convw_ref[i:i + 1, :]
    win_ref[0:_HIST, :] = win_ref[c:c + _HIST, :]
    qkv = _silu(acc)

    row = lax.broadcasted_iota(jnp.int32, (c, c), 0)
    col = lax.broadcasted_iota(jnp.int32, (c, c), 1)
    incl = row >= col
    strict = row > col
    tri_incl = incl.astype(F32)
    tri_upper = (row <= col).astype(F32)

    beta = _sigmoid(ba_ref[:, 0:GDN_HEADS])
    log_g_c = -jnp.exp(alog_r_ref[...]) * _softplus(ba_ref[:, GDN_HEADS:2 * GDN_HEADS] + dtb_r_ref[...])
    log_g_r = -jnp.exp(alog_c_ref[...]) * _softplus(at_ref[...] + dtb_c_ref[...])
    if n_pad:
        valid_c = lax.broadcasted_iota(jnp.int32, (c, GDN_HEADS), 0) >= n_pad
        valid_r = lax.broadcasted_iota(jnp.int32, (GDN_HEADS, c), 1) >= n_pad
        beta = jnp.where(valid_c, beta, 0.0)
        log_g_c = jnp.where(valid_c, log_g_c, 0.0)
        log_g_r = jnp.where(valid_r, log_g_r, 0.0)
    gam_c = _dot(tri_incl, log_g_c, precision=HIGHEST)
    gam_r = _dot(log_g_r, tri_upper, precision=HIGHEST)

    heads = range(GDN_HEADS)
    kn, kn_b, decay, a_mat, rhs, qk_b, wq_b, kdt_b, g_tot = [], [], [], [], [], [], [], [], []
    qn_l, eg_l = [], []
    for h in heads:
        qh = qkv[:, h * GDN_DK:(h + 1) * GDN_DK]
        kh = qkv[:, GDN_QK + h * GDN_DK:GDN_QK + (h + 1) * GDN_DK]
        vh = qkv[:, 2 * GDN_QK + h * GDN_DV:2 * GDN_QK + (h + 1) * GDN_DV]
        qn = qh * lax.rsqrt(jnp.sum(qh * qh, -1, keepdims=True) + RMS_EPS) * (GDN_DK ** -0.5)
        k_n = kh * lax.rsqrt(jnp.sum(kh * kh, -1, keepdims=True) + RMS_EPS)
        b_c = beta[:, h:h + 1]
        g_c = gam_c[:, h:h + 1]
        g_r = gam_r[h:h + 1, :]
        g_last = g_c[c - 1:c, :]
        e_g = jnp.exp(g_c)
        kn.append(k_n)
        kn_b.append(k_n.astype(BF16))
        qn_l.append(qn)
        eg_l.append(e_g)
        decay.append(jnp.where(incl, jnp.exp(jnp.where(incl, g_c - g_r, 0.0)), 0.0))
        rhs.append(jnp.concatenate([vh * b_c, k_n * (b_c * e_g)], axis=-1))
        kdt_b.append((k_n * jnp.exp(g_last - g_c)).T.astype(BF16))
        g_tot.append(jnp.exp(g_last))
    for h in heads:
        kk = _dot_nt(kn_b[h], kn_b[h])
        a_mat.append(jnp.where(strict, kk * decay[h], 0.0) * beta[:, h:h + 1])
        qk_b.append((_dot_nt(qn_l[h].astype(BF16), kn_b[h]) * decay[h]).astype(BF16))

    l_mat = [-a for a in a_mat]
    p_b = [a.astype(BF16) for a in a_mat]
    for _ in range(5):
        p = [_dot(b, b) for b in p_b]
        p_b = [x.astype(BF16) for x in p]
        l_mat = [l + x + _dot(l.astype(BF16), xb) for l, x, xb in zip(l_mat, p, p_b)]
    sol = [r + _dot(l.astype(BF16), r.astype(BF16)) for l, r in zip(l_mat, rhs)]

    state = [s_ref[h] for h in heads]
    for h in heads:
        wq_b.append(jnp.concatenate([sol[h][:, GDN_DV:], qn_l[h] * eg_l[h]], axis=0).astype(BF16))
    p1 = [_dot(wq_b[h], state[h].astype(BF16)) for h in heads]
    u_b = [(sol[h][:, :GDN_DV] - p1[h][:c]).astype(BF16) for h in heads]
    for h in heads:
        s_ref[h] = state[h] * g_tot[h] + _dot(kdt_b[h], u_b[h])
    if not emit_state:
        for h in heads:
            o = p1[h][c:] + _dot(qk_b[h], u_b[h])
            o = o * lax.rsqrt(jnp.mean(o * o, -1, keepdims=True) + RMS_EPS) * normw_ref[...]
            zh = z_ref[:, h * GDN_DV:(h + 1) * GDN_DV].astype(F32)
            o_ref[:, h * GDN_DV:(h + 1) * GDN_DV] = (o * _silu(zh)).astype(o_ref.dtype)

    if emit_state:
        s_out_ref[...] = s_ref[...]


def _gdn(qkv, z, ba, a_t, hist, s0, conv_w_t, a_log, dt_bias, norm_w, bsz, n_chunks, n_pad, emit_state):
    c = CHUNK
    rows = lambda b, n: (b * n_chunks + n, 0)
    c2 = lambda b, n: (0, 0)
    c3 = lambda b, n: (0, 0, 0)
    in_specs = [
        pl.BlockSpec((c, qkv.shape[1]), rows),
        pl.BlockSpec((c, z.shape[1]), rows),
        pl.BlockSpec((c, ba.shape[1]), rows),
        pl.BlockSpec((None, GDN_HEADS, c), lambda b, n: (b * n_chunks + n, 0, 0)),
        pl.BlockSpec(hist.shape, c2),
        pl.BlockSpec(s0.shape, c3),
        pl.BlockSpec(conv_w_t.shape, c2),
        pl.BlockSpec((1, GDN_HEADS), c2),
        pl.BlockSpec((1, GDN_HEADS), c2),
        pl.BlockSpec((GDN_HEADS, 1), c2),
        pl.BlockSpec((GDN_HEADS, 1), c2),
        pl.BlockSpec((1, GDN_DV), c2),
    ]
    if emit_state:
        out_specs = pl.BlockSpec(s0.shape, c3)
        out_shape = jax.ShapeDtypeStruct(s0.shape, F32)
    else:
        out_specs = pl.BlockSpec((c, GDN_VD), rows)
        out_shape = jax.ShapeDtypeStruct((qkv.shape[0], GDN_VD), BF16)
    return pl.pallas_call(
        functools.partial(_gdn_kernel, n_pad=n_pad, emit_state=emit_state),
        grid=(bsz, n_chunks),
        in_specs=in_specs,
        out_specs=out_specs,
        out_shape=out_shape,
        scratch_shapes=[
            pltpu.VMEM((GDN_HEADS, GDN_DK, GDN_DV), F32),
            pltpu.VMEM((_HIST + c, qkv.shape[1]), F32),
        ],
        compiler_params=pltpu.CompilerParams(dimension_semantics=("parallel", "arbitrary"),
                                             vmem_limit_bytes=VMEM_LIMIT),
        name="gdn_meta" if emit_state else "gdn",
    )(qkv, z, ba, a_t, hist, s0, conv_w_t, a_log.reshape(1, -1), dt_bias.reshape(1, -1),
      a_log.reshape(-1, 1), dt_bias.reshape(-1, 1), norm_w.reshape(1, -1))


def _swa_kernel(q_ref, kvc_ref, kvp_ref, kvm_ref, sink_ref, o_ref):
    i = pl.program_id(1)
    blk = WINDOW
    hd = SWA_HEAD_DIM
    rows = SWA_GROUP * blk
    t = lax.broadcasted_iota(jnp.int32, (rows, blk), 0) & (blk - 1)
    s = lax.broadcasted_iota(jnp.int32, (rows, blk), 1)
    d_cur = (t - s).astype(F32)
    m_cur = t >= s
    m_prev = jnp.logical_and(s > t, i > 0)
    tm_ = lax.broadcasted_iota(jnp.int32, (rows, N_META), 0) & (blk - 1)
    sm_ = lax.broadcasted_iota(jnp.int32, (rows, N_META), 1)
    d_meta = (tm_ - sm_ + N_META + i * blk).astype(F32)
    ones = jnp.ones((blk, hd), BF16)
    ones_meta = jnp.ones((N_META, hd), BF16)
    groups = range(SWA_KV_HEADS)

    def per_row(vals):
        return jnp.concatenate([jnp.broadcast_to(v, (blk, 1)) for v in vals], axis=0)

    sc, sinks = [], []
    for g in groups:
        hqs = [g * SWA_GROUP + j for j in range(SWA_GROUP)]
        q_g = jnp.concatenate([q_ref[:, hq * hd:(hq + 1) * hd] for hq in hqs], axis=0) * (hd ** -0.5)
        slope = per_row([jnp.full((1, 1), 2.0 ** (-8.0 * (hq + 1) / SWA_Q_HEADS), F32) for hq in hqs])
        bias_cur = slope * d_cur
        sc_cur = jnp.where(m_cur, _dot_nt(q_g, kvc_ref[:, g * hd:(g + 1) * hd]) - bias_cur, NEG_BIG)
        sc_prev = jnp.where(m_prev, _dot_nt(q_g, kvp_ref[:, g * hd:(g + 1) * hd]) - (bias_cur + slope * blk),
                            NEG_BIG)
        sc_meta = _dot_nt(q_g, kvm_ref[:, g * hd:(g + 1) * hd]) - slope * d_meta
        sc.append((sc_cur, sc_prev, sc_meta))
        sinks.append(per_row([sink_ref[:, hq:hq + 1] for hq in hqs]))
    e, e_sink = [], []
    for g in groups:
        sc_cur, sc_prev, sc_meta = sc[g]
        m = jnp.maximum(jnp.maximum(jnp.max(jnp.maximum(sc_cur, sc_prev), -1, keepdims=True),
                                    jnp.max(sc_meta, -1, keepdims=True)), sinks[g])
        e.append(tuple(jnp.exp(x - m).astype(BF16) for x in sc[g]))
        e_sink.append(jnp.exp(sinks[g] - m))
    for g in groups:
        v_of = lambda ref, pad: jnp.concatenate([ref[:, SWA_KVD + g * hd:SWA_KVD + (g + 1) * hd], pad], axis=-1)
        acc = (_dot(e[g][0], v_of(kvc_ref, ones)) + _dot(e[g][1], v_of(kvp_ref, ones))
               + _dot(e[g][2], v_of(kvm_ref, ones_meta)))
        o = acc[:, :hd] / (acc[:, hd:hd + 1] + e_sink[g])
        for j in range(SWA_GROUP):
            hq = g * SWA_GROUP + j
            o_ref[:, hq * hd:(hq + 1) * hd] = o[j * blk:(j + 1) * blk].astype(o_ref.dtype)


def _swa(q, kv, kv_meta, sinks, bsz, nq):
    blk = WINDOW
    return pl.pallas_call(
        _swa_kernel,
        grid=(bsz, nq),
        in_specs=[
            pl.BlockSpec((blk, SWA_QD), lambda b, i: (b * nq + i, 0)),
            pl.BlockSpec((blk, 2 * SWA_KVD), lambda b, i: (b * nq + i, 0)),
            pl.BlockSpec((blk, 2 * SWA_KVD), lambda b, i: (b * nq + jnp.maximum(i - 1, 0), 0)),
            pl.BlockSpec((N_META, 2 * SWA_KVD), lambda b, i: (0, 0)),
            pl.BlockSpec((1, SWA_Q_HEADS), lambda b, i: (0, 0)),
        ],
        out_specs=pl.BlockSpec((blk, SWA_QD), lambda b, i: (b * nq + i, 0)),
        out_shape=jax.ShapeDtypeStruct((q.shape[0], SWA_QD), BF16),
        compiler_params=pltpu.CompilerParams(dimension_semantics=("parallel", "parallel"),
                                             vmem_limit_bytes=VMEM_LIMIT),
        name="swa",
    )(q, kv, kv, kv_meta, sinks.reshape(1, -1))


def _merge_kernel(x_ref, eg_ref, eb_ref, og_ref, os_ref, gates_ref, wg_ref, ws_ref, wo_ref, g1_ref, b1_ref,
                  wr_ref, br_ref, h1_ref, idx_ref, tw_ref):
    d = x_ref.shape[1]
    h0 = _layer_norm(x_ref[...], eg_ref[...], eb_ref[...])
    y_g = _dot(og_ref[...], wg_ref[...])
    y_s = _dot(os_ref[...], ws_ref[...])
    mix = _sigmoid(gates_ref[:, :d].astype(F32)) * y_g + _sigmoid(gates_ref[:, d:].astype(F32)) * y_s
    mixed = _dot(mix.astype(BF16), wo_ref[...])
    h1 = _layer_norm(DEEPNORM_ALPHA * h0 + mixed, g1_ref[...], b1_ref[...])
    h1_ref[...] = h1

    logits = _dot(h1, wr_ref[...], precision=HIGHEST) + br_ref[...]
    lane = lax.broadcasted_iota(jnp.int32, logits.shape, 1)
    vals, idxs = [], []
    for _ in range(TOP_K):
        m = jnp.max(logits, -1, keepdims=True)
        sel = jnp.min(jnp.where(logits == m, lane, LANES), -1, keepdims=True)
        vals.append(m)
        idxs.append(sel)
        logits = jnp.where(lane == sel, NEG_BIG, logits)
    e = [jnp.exp(v - vals[0]) for v in vals]
    tot = e[0] + e[1] + e[2] + e[3]
    for k in range(TOP_K):
        idx_ref[:, k:k + 1] = idxs[k]
        tw_ref[:, k:k + 1] = e[k] / tot


def _merge(x2d, eg, eb, o_gdn, o_swa, gates, wg, ws, wo, g1, b1, wr, br, tm):
    n, d = x2d.shape
    row = lambda i: (i, 0)
    const = lambda i: (0, 0)
    full = lambda a: pl.BlockSpec(a.shape, const)
    return pl.pallas_call(
        _merge_kernel,
        grid=(n // tm,),
        in_specs=[
            pl.BlockSpec((tm, d), row), full(eg), full(eb),
            pl.BlockSpec((tm, GDN_VD), row), pl.BlockSpec((tm, SWA_QD), row), pl.BlockSpec((tm, 2 * d), row),
            full(wg), full(ws), full(wo), full(g1), full(b1), full(wr), full(br),
        ],
        out_specs=[
            pl.BlockSpec((tm, d), row), pl.BlockSpec((tm, TOP_K), row), pl.BlockSpec((tm, TOP_K), row),
        ],
        out_shape=[
            jax.ShapeDtypeStruct((n, d), F32),
            jax.ShapeDtypeStruct((n, TOP_K), jnp.int32), jax.ShapeDtypeStruct((n, TOP_K), F32),
        ],
        compiler_params=pltpu.CompilerParams(dimension_semantics=("parallel",), vmem_limit_bytes=VMEM_LIMIT),
        name="merge",
    )(x2d, eg, eb, o_gdn, o_swa, gates, wg, ws, wo, g1, b1, wr, br)


def _rank_kernel(idx_ref, rank_ref, counts_ref, run_ref):
    t = idx_ref.shape[0]

    @pl.when(pl.program_id(0) == 0)
    def _():
        run_ref[...] = jnp.zeros_like(run_ref)

    lane = lax.broadcasted_iota(jnp.int32, (t, LANES), 1)
    hits = [idx_ref[:, k:k + 1] == lane for k in range(TOP_K)]
    onehot = hits[0].astype(F32)
    for k in range(1, TOP_K):
        onehot = onehot + hits[k].astype(F32)
    r = lax.broadcasted_iota(jnp.int32, (t, t), 0)
    c = lax.broadcasted_iota(jnp.int32, (t, t), 1)
    before = (r > c).astype(BF16)
    base = run_ref[...] + _dot(before, onehot.astype(BF16))
    for k in range(TOP_K):
        rank_ref[:, k:k + 1] = jnp.sum(jnp.where(hits[k], base, 0.0), -1, keepdims=True).astype(jnp.int32)
    run_ref[...] = run_ref[...] + jnp.sum(onehot, 0, keepdims=True)
    counts_ref[...] = run_ref[...].astype(jnp.int32)


def _rank(top_idx, t):
    n = top_idx.shape[0]
    return pl.pallas_call(
        _rank_kernel,
        grid=(n // t,),
        in_specs=[pl.BlockSpec((t, TOP_K), lambda i: (i, 0))],
        out_specs=[pl.BlockSpec((t, TOP_K), lambda i: (i, 0)), pl.BlockSpec((1, LANES), lambda i: (0, 0))],
        out_shape=[jax.ShapeDtypeStruct((n, TOP_K), jnp.int32), jax.ShapeDtypeStruct((1, LANES), jnp.int32)],
        scratch_shapes=[pltpu.VMEM((1, LANES), F32)],
        compiler_params=pltpu.CompilerParams(dimension_semantics=("arbitrary",)),
        name="rank",
    )(top_idx)


_ROW_GROUP = 8


def _row_copy(src, dst, sem):
    return pltpu.make_async_copy(src, dst, sem)


def _dispatch_kernel(pad_end_ref, dest_hbm, h_ref, xs_hbm, idx_smem, zeros_ref, sem_idx, sem_rows, sem_zero):
    i = pl.program_id(0)
    t = h_ref.shape[0]
    n_idx = t * TOP_K

    @pl.when(i == 0)
    def _():
        zeros_ref[...] = jnp.zeros_like(zeros_ref)
        for e in range(N_EXPERTS):
            end = pad_end_ref[e]
            start = pad_end_ref[e - 1] if e else 0

            @pl.when(end > start)
            def _():
                tile0 = pl.multiple_of(end - MOE_TILE, MOE_TILE)
                fill = _row_copy(zeros_ref, xs_hbm.at[pl.ds(tile0, MOE_TILE)], sem_zero)
                fill.start()
                fill.wait()

    idx_cp = _row_copy(dest_hbm.at[pl.ds(pl.multiple_of(i * n_idx, n_idx), n_idx)], idx_smem, sem_idx)
    idx_cp.start()
    idx_cp.wait()

    def issue(grp, carry):
        row0 = pl.multiple_of(grp * _ROW_GROUP, _ROW_GROUP)
        for r in range(_ROW_GROUP):
            for k in range(TOP_K):
                dst_row = idx_smem[(row0 + r) * TOP_K + k]
                _row_copy(h_ref.at[pl.ds(row0 + r, 1)], xs_hbm.at[pl.ds(dst_row, 1)],
                          sem_rows).start(priority=k % 2)
        return carry

    lax.fori_loop(0, t // _ROW_GROUP, issue, 0)
    _row_copy(xs_hbm.at[pl.ds(0, n_idx)], xs_hbm.at[pl.ds(0, n_idx)], sem_rows).wait()


def _dispatch(pad_end, dest_flat, h1, n_rows, t):
    n, d = h1.shape
    grid_spec = pltpu.PrefetchScalarGridSpec(
        num_scalar_prefetch=1,
        grid=(n // t,),
        in_specs=[
            pl.BlockSpec(memory_space=pl.ANY),
            pl.BlockSpec((t, d), lambda i, pe: (i, 0)),
        ],
        out_specs=pl.BlockSpec(memory_space=pl.ANY),
        scratch_shapes=[
            pltpu.SMEM((t * TOP_K,), jnp.int32),
            pltpu.VMEM((MOE_TILE, d), h1.dtype),
            pltpu.SemaphoreType.DMA, pltpu.SemaphoreType.DMA, pltpu.SemaphoreType.DMA,
        ],
    )
    return pl.pallas_call(
        _dispatch_kernel,
        grid_spec=grid_spec,
        out_shape=jax.ShapeDtypeStruct((n_rows, d), h1.dtype),
        compiler_params=pltpu.CompilerParams(dimension_semantics=("arbitrary",), vmem_limit_bytes=VMEM_LIMIT),
        name="dispatch",
    )(pad_end, dest_flat, h1)


def _expert_kernel(blk_e_ref, n_used_ref, xs_ref, w1_ref, b1_ref, w2_ref, b2_ref, ys_ref):
    del blk_e_ref
    i = pl.program_id(0)
    d_ff = w2_ref.shape[0]

    @pl.when(i < n_used_ref[0])
    def _():
        hdn = _dot(xs_ref[...].astype(BF16), w1_ref[...]) + b1_ref[...]
        gate = jnp.minimum(hdn[:, :d_ff], SWIGLU_LIMIT)
        up = jnp.clip(hdn[:, d_ff:], -SWIGLU_LIMIT, SWIGLU_LIMIT)
        act = gate * _sigmoid(SWIGLU_ALPHA * gate) * (up + 1.0)
        ys_ref[...] = _dot(act.astype(BF16), w2_ref[...]) + b2_ref[...]

    @pl.when(i >= n_used_ref[0])
    def _():
        ys_ref[...] = jnp.zeros_like(ys_ref)


def _experts(blk_e, n_used, xs, w1, b1, w2, b2):
    n_rows, w = xs.shape
    n_e, d, d_ff2 = w1.shape
    d_ff = w2.shape[1]
    grid_spec = pltpu.PrefetchScalarGridSpec(
        num_scalar_prefetch=2,
        grid=(n_rows // MOE_TILE,),
        in_specs=[
            pl.BlockSpec((MOE_TILE, w), lambda i, be, nu: (jnp.minimum(i, nu[0] - 1), 0)),
            pl.BlockSpec((None, d, d_ff2), lambda i, be, nu: (be[i], 0, 0)),
            pl.BlockSpec((None, 1, d_ff2), lambda i, be, nu: (be[i], 0, 0)),
            pl.BlockSpec((None, d_ff, d), lambda i, be, nu: (be[i], 0, 0)),
            pl.BlockSpec((None, 1, d), lambda i, be, nu: (be[i], 0, 0)),
        ],
        out_specs=pl.BlockSpec((MOE_TILE, w), lambda i, be, nu: (i, 0)),
    )
    return pl.pallas_call(
        _expert_kernel,
        grid_spec=grid_spec,
        out_shape=jax.ShapeDtypeStruct((n_rows, w), F32),
        compiler_params=pltpu.CompilerParams(dimension_semantics=("arbitrary",), vmem_limit_bytes=VMEM_LIMIT),
        name="experts",
    )(blk_e, n_used, xs, w1, b1.reshape(n_e, 1, d_ff2), w2, b2.reshape(n_e, 1, d))


def _combine_kernel(dest_hbm, ys_hbm, h1_ref, tw_ref, g_ref, b_ref, out_ref, idx_smem, buf, sem_idx, sem_rows):
    i = pl.program_id(0)
    t = h1_ref.shape[0]
    n_idx = t * TOP_K

    def start_gather(step, slot):
        idx_cp = _row_copy(dest_hbm.at[pl.ds(pl.multiple_of(step * n_idx, n_idx), n_idx)], idx_smem, sem_idx)
        idx_cp.start()
        idx_cp.wait()

        def issue(grp, carry):
            row0 = pl.multiple_of(grp * _ROW_GROUP, _ROW_GROUP)
            for r in range(_ROW_GROUP):
                for k in range(TOP_K):
                    src_row = idx_smem[(row0 + r) * TOP_K + k]
                    _row_copy(ys_hbm.at[pl.ds(src_row, 1)], buf.at[slot, k, pl.ds(row0 + r, 1)],
                              sem_rows.at[slot]).start(priority=k % 2)
            return carry

        lax.fori_loop(0, t // _ROW_GROUP, issue, 0)

    @pl.when(i == 0)
    def _():
        start_gather(0, 0)

    @pl.when(i + 1 < pl.num_programs(0))
    def _():
        start_gather(i + 1, (i + 1) % 2)

    slot = i % 2
    _row_copy(buf.at[slot], buf.at[slot], sem_rows.at[slot]).wait()

    ffn = None
    for k in range(TOP_K):
        term = buf[slot, k] * tw_ref[:, k:k + 1]
        ffn = term if ffn is None else ffn + term
    out_ref[...] = _layer_norm(DEEPNORM_ALPHA * h1_ref[...] + ffn, g_ref[...], b_ref[...])


def _combine(dest_flat, ys, h1, top_w, g2, b2, t):
    n, d = h1.shape
    w = ys.shape[1]
    return pl.pallas_call(
        _combine_kernel,
        grid=(n // t,),
        in_specs=[
            pl.BlockSpec(memory_space=pl.ANY),
            pl.BlockSpec(memory_space=pl.ANY),
            pl.BlockSpec((t, d), lambda i: (i, 0)),
            pl.BlockSpec((t, TOP_K), lambda i: (i, 0)),
            pl.BlockSpec((1, d), lambda i: (0, 0)),
            pl.BlockSpec((1, d), lambda i: (0, 0)),
        ],
        out_specs=pl.BlockSpec((t, d), lambda i: (i, 0)),
        out_shape=jax.ShapeDtypeStruct((n, d), F32),
        scratch_shapes=[
            pltpu.SMEM((t * TOP_K,), jnp.int32),
            pltpu.VMEM((2, TOP_K, t, w), ys.dtype),
            pltpu.SemaphoreType.DMA,
            pltpu.SemaphoreType.DMA((2,)),
        ],
        compiler_params=pltpu.CompilerParams(dimension_semantics=("arbitrary",), vmem_limit_bytes=VMEM_LIMIT),
        name="combine",
    )(dest_flat, ys, h1, top_w, g2, b2)


def _largest_tile(n, cap):
    t = cap
    while n % t:
        t //= 2
    return t


def kernel(x, meta_tokens, emb_ln_g, emb_ln_b, w_in, conv_w, a_log, dt_bias, gdn_norm_w, attn_sinks, w_br_gdn,
           w_br_swa, w_out, ln1_g, ln1_b, w_router, b_router, w_moe1, b_moe1, w_moe2, b_moe2, ln2_g, ln2_b):
    bsz, seq, d = x.shape
    assert seq % WINDOW == 0 and seq % CHUNK == 0 and d == GDN_VD
    n = bsz * seq
    x2d = x.reshape(n, d)
    row1 = lambda v: v.reshape(1, -1)
    l = 0

    sp = np.cumsum((GDN_QK, GDN_QK, GDN_VD, GDN_VD, GDN_HEADS, GDN_HEADS, SWA_QD, SWA_KVD, SWA_KVD, d, d))
    w = w_in[l]
    w_perm = jnp.concatenate([
        w[:, :sp[3]], w[:, sp[5]:sp[10]], w[:, sp[3]:sp[5]],
        jnp.zeros((d, LANES - 2 * GDN_HEADS), w.dtype)], axis=1).astype(BF16)
    eg, eb = row1(emb_ln_g), row1(emb_ln_b)

    tm = _largest_tile(n, 512)
    qkv_g, z_g, q_s, kv_s, gates, ba = _in_proj(x2d, eg, eb, w_perm, tm)
    m_qkv, _, _, m_kv, _, m_ba = _in_proj(meta_tokens.astype(x.dtype), eg, eb, w_perm, N_META)

    n_pad = CHUNK - N_META
    n_chunks = seq // CHUNK
    conv_w_t = conv_w[l].T
    pad_rows = lambda a: jnp.pad(a, ((n_pad, 0), (0, 0)))
    chunk_t = lambda a, nb: a[:, GDN_HEADS:2 * GDN_HEADS].reshape(nb, CHUNK, GDN_HEADS).transpose(0, 2, 1)
    m_ba_p = pad_rows(m_ba)
    zeros_state = jnp.zeros((GDN_HEADS, GDN_DK, GDN_DV), F32)
    s_meta = _gdn(pad_rows(m_qkv), jnp.zeros((CHUNK, GDN_VD), BF16), m_ba_p, chunk_t(m_ba_p, 1),
                  jnp.zeros((_HIST, m_qkv.shape[1]), BF16), zeros_state, conv_w_t, a_log[l], dt_bias[l],
                  gdn_norm_w[l], 1, 1, n_pad, True)
    o_gdn = _gdn(qkv_g, z_g, ba, chunk_t(ba, bsz * n_chunks), m_qkv[N_META - _HIST:], s_meta, conv_w_t,
                 a_log[l], dt_bias[l], gdn_norm_w[l], bsz, n_chunks, 0, False)

    o_swa = _swa(q_s, kv_s, m_kv, attn_sinks[l], bsz, seq // WINDOW)

    wr = jnp.pad(w_router[l], ((0, 0), (0, LANES - N_EXPERTS)))
    br = jnp.pad(row1(b_router[l]), ((0, 0), (0, LANES - N_EXPERTS)), constant_values=NEG_BIG)
    h1, top_idx, top_w = _merge(
        x2d, eg, eb, o_gdn, o_swa, gates, w_br_gdn[l].astype(BF16), w_br_swa[l].astype(BF16),
        w_out[l].astype(BF16), row1(ln1_g[l]), row1(ln1_b[l]), wr, br, tm)

    rank, counts = _rank(top_idx, _largest_tile(n, 512))
    counts = counts[0, :N_EXPERTS]
    padded = (counts + MOE_TILE - 1) // MOE_TILE * MOE_TILE
    pad_end = jnp.cumsum(padded)
    pad_start = pad_end - padded
    dest = (pad_start[top_idx] + rank).reshape(-1)
    n_blk = -(-n * TOP_K // MOE_TILE) + N_EXPERTS
    blk_row0 = jnp.arange(n_blk, dtype=jnp.int32) * MOE_TILE
    blk_e = jnp.minimum(jnp.sum((pad_end[None, :] <= blk_row0[:, None]).astype(jnp.int32), axis=1), N_EXPERTS - 1)
    n_used = (pad_end[-1:] // MOE_TILE).astype(jnp.int32)

    tg = _largest_tile(n, GATHER_TOKENS)
    xs = _dispatch(pad_end.astype(jnp.int32), dest, h1, n_blk * MOE_TILE, tg)
    ys = _experts(blk_e, n_used, xs, w_moe1[l].astype(BF16), b_moe1[l], w_moe2[l].astype(BF16), b_moe2[l])
    out = _combine(dest, ys, h1, top_w, row1(ln2_g[l]), row1(ln2_b[l]), tg)
    return out.reshape(bsz, seq, d)
```

```python
import functools

import jax
import jax.numpy as jnp
import numpy as np
from jax import lax
from jax.experimental import pallas as pl
from jax.experimental.pallas import tpu as pltpu

N_META = 16
GDN_HEADS = 8
GDN_DK = 128
GDN_DV = 128
CONV_WIDTH = 4
CHUNK = 64
SWA_Q_HEADS = 16
SWA_KV_HEADS = 4
SWA_HEAD_DIM = 64
SWA_GROUP = SWA_Q_HEADS // SWA_KV_HEADS
WINDOW = 128
N_EXPERTS = 32
TOP_K = 4
SWIGLU_ALPHA = 1.702
SWIGLU_LIMIT = 7.0
LN_EPS = 1e-5
RMS_EPS = 1e-6
DEPTH = 1
DEEPNORM_ALPHA = (2.0 * DEPTH) ** 0.25

GDN_QK = GDN_HEADS * GDN_DK
GDN_VD = GDN_HEADS * GDN_DV
SWA_QD = SWA_Q_HEADS * SWA_HEAD_DIM
SWA_KVD = SWA_KV_HEADS * SWA_HEAD_DIM

LANES = 128
MOE_TILE = 512
GATHER_TOKENS = 512
NEG_BIG = -1e30
VMEM_LIMIT = 56 * 1024 * 1024

F32 = jnp.float32
BF16 = jnp.bfloat16
HIGHEST = lax.Precision.HIGHEST


def _layer_norm(x, g, b):
    mu = jnp.mean(x, -1, keepdims=True)
    xc = x - mu
    var = jnp.mean(xc * xc, -1, keepdims=True)
    return xc * lax.rsqrt(var + LN_EPS) * g + b


def _sigmoid(x):
    return 1.0 / (1.0 + jnp.exp(-x))


def _silu(x):
    return x * _sigmoid(x)


def _softplus(x):
    return jnp.maximum(x, 0.0) + jnp.log(1.0 + jnp.exp(-jnp.abs(x)))


def _dot(a, b, **kw):
    return jnp.dot(a, b, preferred_element_type=F32, **kw)


def _dot_nt(a, b):
    return lax.dot_general(a, b, (((1,), (1,)), ((), ())), preferred_element_type=F32)


_C_QKVG = 2 * GDN_QK + GDN_VD
_C_Z = _C_QKVG + GDN_VD
_C_QS = _C_Z + SWA_QD
_C_KVS = _C_QS + 2 * SWA_KVD
_C_GATES = _C_KVS + 2 * GDN_VD
_C_BA = _C_GATES + LANES


def _inproj_kernel(x_ref, g_ref, b_ref, w_ref, qkvg_ref, z_ref, qs_ref, kvs_ref, gates_ref, ba_ref):
    h = _layer_norm(x_ref[...], g_ref[...], b_ref[...]).astype(BF16)

    def proj(out_ref, lo, hi, step=1024):
        for c in range(lo, hi, step):
            ce = min(c + step, hi)
            out_ref[:, c - lo:ce - lo] = _dot(h, w_ref[:, c:ce]).astype(out_ref.dtype)

    proj(qkvg_ref, 0, _C_QKVG)
    proj(z_ref, _C_QKVG, _C_Z)
    proj(qs_ref, _C_Z, _C_QS)
    proj(kvs_ref, _C_QS, _C_KVS)
    proj(gates_ref, _C_KVS, _C_GATES)
    proj(ba_ref, _C_GATES, _C_BA)


def _in_proj(x2d, ln_g, ln_b, w_perm, tm):
    n, d = x2d.shape
    widths = (_C_QKVG, GDN_VD, SWA_QD, 2 * SWA_KVD, _C_GATES - _C_KVS, LANES)
    dtypes = (BF16, BF16, BF16, BF16, BF16, F32)
    row = lambda i: (i, 0)
    const = lambda i: (0, 0)
    return pl.pallas_call(
        _inproj_kernel,
        grid=(n // tm,),
        in_specs=[
            pl.BlockSpec((tm, d), row),
            pl.BlockSpec((1, d), const),
            pl.BlockSpec((1, d), const),
            pl.BlockSpec((d, _C_BA), const, pipeline_mode=pl.Buffered(1)),
        ],
        out_specs=[pl.BlockSpec((tm, w), row) for w in widths],
        out_shape=[jax.ShapeDtypeStruct((n, w), dt) for w, dt in zip(widths, dtypes)],
        compiler_params=pltpu.CompilerParams(dimension_semantics=("parallel",), vmem_limit_bytes=VMEM_LIMIT),
        name="in_proj",
    )(x2d, ln_g, ln_b, w_perm)


_HIST = 8
GDN_CHUNKS_PER_STEP = 4


def _gdn_kernel(qkv_ref, z_ref, ba_ref, at_ref, hist_ref, s0_ref, convw_ref, alog_r_ref, dtb_r_ref,
                alog_c_ref, dtb_c_ref, normw_ref, out_ref, s_ref, win_ref, *, n_pad, emit_state):
    chunks_per_step = qkv_ref.shape[0] // CHUNK

    @pl.when(pl.program_id(1) == 0)
    def _():
        s_ref[...] = s0_ref[...]
        win_ref[0:_HIST, :] = hist_ref[...].astype(F32)

    def chunk(j, carry):
        _gdn_chunk(j, qkv_ref, z_ref, ba_ref, at_ref, convw_ref, alog_r_ref, dtb_r_ref, alog_c_ref, dtb_c_ref,
                   normw_ref, None if emit_state else out_ref, s_ref, win_ref, n_pad)
        return carry

    if chunks_per_step == 1:
        chunk(0, 0)
    else:
        lax.fori_loop(0, chunks_per_step, chunk, 0)
    if emit_state:
        out_ref[...] = s_ref[...]


def _gdn_chunk(j, qkv_ref, z_ref, ba_ref, at_ref, convw_ref, alog_r_ref, dtb_r_ref, alog_c_ref, dtb_c_ref,
               normw_ref, o_ref, s_ref, win_ref, n_pad):
    c = CHUNK
    rows = pl.ds(j * c, c) if isinstance(j, int) else pl.ds(pl.multiple_of(j * c, c), c)

    win_ref[_HIST:_HIST + c, :] = qkv_ref[rows, :].astype(F32)
    acc = win_ref[_HIST:_HIST + c, :] * convw_ref[CONV_WIDTH - 1:CONV_WIDTH, :]
    for i in range(CONV_WIDTH - 1):
        sh = CONV_WIDTH - 1 - i
        acc = acc + win_ref[_HIST - sh:_HIST - sh + c, :] * convw_ref[i:i + 1, :]
    win_ref[0:_HIST, :] = win_ref[c:c + _HIST, :]
    qkv = _silu(acc)

    row = lax.broadcasted_iota(jnp.int32, (c, c), 0)
    col = lax.broadcasted_iota(jnp.int32, (c, c), 1)
    incl = row >= col
    strict = row > col
    tri_incl = incl.astype(F32)
    tri_upper = (row <= col).astype(F32)

    beta = _sigmoid(ba_ref[rows, 0:GDN_HEADS])
    log_g_c = -jnp.exp(alog_r_ref[...]) * _softplus(ba_ref[rows, GDN_HEADS:2 * GDN_HEADS] + dtb_r_ref[...])
    log_g_r = -jnp.exp(alog_c_ref[...]) * _softplus(at_ref[j] + dtb_c_ref[...])
    if n_pad:
        valid_c = lax.broadcasted_iota(jnp.int32, (c, GDN_HEADS), 0) >= n_pad
        valid_r = lax.broadcasted_iota(jnp.int32, (GDN_HEADS, c), 1) >= n_pad
        beta = jnp.where(valid_c, beta, 0.0)
        log_g_c = jnp.where(valid_c, log_g_c, 0.0)
        log_g_r = jnp.where(valid_r, log_g_r, 0.0)
    gam_c = _dot(tri_incl, log_g_c, precision=HIGHEST)
    gam_r = _dot(log_g_r, tri_upper, precision=HIGHEST)

    heads = range(GDN_HEADS)
    kn, kn_b, decay, a_mat, rhs, qk_b, wq_b, kdt_b, g_tot = [], [], [], [], [], [], [], [], []
    qn_l, eg_l = [], []
    for h in heads:
        qh = qkv[:, h * GDN_DK:(h + 1) * GDN_DK]
        kh = qkv[:, GDN_QK + h * GDN_DK:GDN_QK + (h + 1) * GDN_DK]
        vh = qkv[:, 2 * GDN_QK + h * GDN_DV:2 * GDN_QK + (h + 1) * GDN_DV]
        qn = qh * lax.rsqrt(jnp.sum(qh * qh, -1, keepdims=True) + RMS_EPS) * (GDN_DK ** -0.5)
        k_n = kh * lax.rsqrt(jnp.sum(kh * kh, -1, keepdims=True) + RMS_EPS)
        b_c = beta[:, h:h + 1]
        g_c = gam_c[:, h:h + 1]
        g_r = gam_r[h:h + 1, :]
        g_last = g_c[c - 1:c, :]
        e_g = jnp.exp(g_c)
        kn.append(k_n)
        kn_b.append(k_n.astype(BF16))
        qn_l.append(qn)
        eg_l.append(e_g)
        decay.append(jnp.where(incl, jnp.exp(jnp.where(incl, g_c - g_r, 0.0)), 0.0))
        rhs.append(jnp.concatenate([vh * b_c, k_n * (b_c * e_g)], axis=-1))
        kdt_b.append((k_n * jnp.exp(g_last - g_c)).T.astype(BF16))
        g_tot.append(jnp.exp(g_last))
    for h in heads:
        kk = _dot_nt(kn_b[h], kn_b[h])
        a_mat.append(jnp.where(strict, kk * decay[h], 0.0) * beta[:, h:h + 1])
        qk_b.append((_dot_nt(qn_l[h].astype(BF16), kn_b[h]) * decay[h]).astype(BF16))

    l_mat = [-a for a in a_mat]
    p_b = [a.astype(BF16) for a in a_mat]
    for _ in range(5):
        p = [_dot(b, b) for b in p_b]
        p_b = [x.astype(BF16) for x in p]
        l_mat = [l + x + _dot(l.astype(BF16), xb) for l, x, xb in zip(l_mat, p, p_b)]
    sol = [r + _dot(l.astype(BF16), r.astype(BF16)) for l, r in zip(l_mat, rhs)]

    state = [s_ref[h] for h in heads]
    for h in heads:
        wq_b.append(jnp.concatenate([sol[h][:, GDN_DV:], qn_l[h] * eg_l[h]], axis=0).astype(BF16))
    p1 = [_dot(wq_b[h], state[h].astype(BF16)) for h in heads]
    u_b = [(sol[h][:, :GDN_DV] - p1[h][:c]).astype(BF16) for h in heads]
    for h in heads:
        s_ref[h] = state[h] * g_tot[h] + _dot(kdt_b[h], u_b[h])
    if o_ref is not None:
        for h in heads:
            o = p1[h][c:] + _dot(qk_b[h], u_b[h])
            o = o * lax.rsqrt(jnp.mean(o * o, -1, keepdims=True) + RMS_EPS) * normw_ref[...]
            zh = z_ref[rows, h * GDN_DV:(h + 1) * GDN_DV].astype(F32)
            o_ref[rows, h * GDN_DV:(h + 1) * GDN_DV] = (o * _silu(zh)).astype(o_ref.dtype)


def _gdn(qkv, z, ba, a_t, hist, s0, conv_w_t, a_log, dt_bias, norm_w, bsz, n_chunks, n_pad, emit_state):
    cps = _largest_tile(n_chunks, GDN_CHUNKS_PER_STEP)
    n_steps = n_chunks // cps
    c = cps * CHUNK
    rows = lambda b, n: (b * n_steps + n, 0)
    c2 = lambda b, n: (0, 0)
    c3 = lambda b, n: (0, 0, 0)
    in_specs = [
        pl.BlockSpec((c, qkv.shape[1]), rows),
        pl.BlockSpec((c, z.shape[1]), rows),
        pl.BlockSpec((c, ba.shape[1]), rows),
        pl.BlockSpec((cps, GDN_HEADS, CHUNK), lambda b, n: (b * n_steps + n, 0, 0)),
        pl.BlockSpec(hist.shape, c2),
        pl.BlockSpec(s0.shape, c3),
        pl.BlockSpec(conv_w_t.shape, c2),
        pl.BlockSpec((1, GDN_HEADS), c2),
        pl.BlockSpec((1, GDN_HEADS), c2),
        pl.BlockSpec((GDN_HEADS, 1), c2),
        pl.BlockSpec((GDN_HEADS, 1), c2),
        pl.BlockSpec((1, GDN_DV), c2),
    ]
    if emit_state:
        out_specs = pl.BlockSpec(s0.shape, c3)
        out_shape = jax.ShapeDtypeStruct(s0.shape, F32)
    else:
        out_specs = pl.BlockSpec((c, GDN_VD), rows)
        out_shape = jax.ShapeDtypeStruct((qkv.shape[0], GDN_VD), BF16)
    return pl.pallas_call(
        functools.partial(_gdn_kernel, n_pad=n_pad, emit_state=emit_state),
        grid=(bsz, n_steps),
        in_specs=in_specs,
        out_specs=out_specs,
        out_shape=out_shape,
        scratch_shapes=[
            pltpu.VMEM((GDN_HEADS, GDN_DK, GDN_DV), F32),
            pltpu.VMEM((_HIST + CHUNK, qkv.shape[1]), F32),
        ],
        compiler_params=pltpu.CompilerParams(dimension_semantics=("parallel", "arbitrary"),
                                             vmem_limit_bytes=VMEM_LIMIT),
        name="gdn_meta" if emit_state else "gdn",
    )(qkv, z, ba, a_t, hist, s0, conv_w_t, a_log.reshape(1, -1), dt_bias.reshape(1, -1),
      a_log.reshape(-1, 1), dt_bias.reshape(-1, 1), norm_w.reshape(1, -1))


def _swa_kernel(q_ref, kvc_ref, kvp_ref, kvm_ref, sink_ref, o_ref):
    i = pl.program_id(1)
    blk = WINDOW
    hd = SWA_HEAD_DIM
    rows = SWA_GROUP * blk
    t = lax.broadcasted_iota(jnp.int32, (rows, blk), 0) & (blk - 1)
    s = lax.broadcasted_iota(jnp.int32, (rows, blk), 1)
    d_cur = (t - s).astype(F32)
    m_cur = t >= s
    m_prev = jnp.logical_and(s > t, i > 0)
    tm_ = lax.broadcasted_iota(jnp.int32, (rows, N_META), 0) & (blk - 1)
    sm_ = lax.broadcasted_iota(jnp.int32, (rows, N_META), 1)
    d_meta = (tm_ - sm_ + N_META + i * blk).astype(F32)
    ones = jnp.ones((blk, hd), BF16)
    ones_meta = jnp.ones((N_META, hd), BF16)
    groups = range(SWA_KV_HEADS)

    def per_row(vals):
        return jnp.concatenate([jnp.broadcast_to(v, (blk, 1)) for v in vals], axis=0)

    sc, sinks = [], []
    for g in groups:
        hqs = [g * SWA_GROUP + j for j in range(SWA_GROUP)]
        q_g = jnp.concatenate([q_ref[:, hq * hd:(hq + 1) * hd] for hq in hqs], axis=0) * (hd ** -0.5)
        slope = per_row([jnp.full((1, 1), 2.0 ** (-8.0 * (hq + 1) / SWA_Q_HEADS), F32) for hq in hqs])
        bias_cur = slope * d_cur
        sc_cur = jnp.where(m_cur, _dot_nt(q_g, kvc_ref[:, g * hd:(g + 1) * hd]) - bias_cur, NEG_BIG)
        sc_prev = jnp.where(m_prev, _dot_nt(q_g, kvp_ref[:, g * hd:(g + 1) * hd]) - (bias_cur + slope * blk),
                            NEG_BIG)
        sc_meta = _dot_nt(q_g, kvm_ref[:, g * hd:(g + 1) * hd]) - slope * d_meta
        sc.append((sc_cur, sc_prev, sc_meta))
        sinks.append(per_row([sink_ref[:, hq:hq + 1] for hq in hqs]))
    e, e_sink = [], []
    for g in groups:
        sc_cur, sc_prev, sc_meta = sc[g]
        m = jnp.maximum(jnp.maximum(jnp.max(jnp.maximum(sc_cur, sc_prev), -1, keepdims=True),
                                    jnp.max(sc_meta, -1, keepdims=True)), sinks[g])
        e.append(tuple(jnp.exp(x - m).astype(BF16) for x in sc[g]))
        e_sink.append(jnp.exp(sinks[g] - m))
    for g in groups:
        v_of = lambda ref, pad: jnp.concatenate([ref[:, SWA_KVD + g * hd:SWA_KVD + (g + 1) * hd], pad], axis=-1)
        acc = (_dot(e[g][0], v_of(kvc_ref, ones)) + _dot(e[g][1], v_of(kvp_ref, ones))
               + _dot(e[g][2], v_of(kvm_ref, ones_meta)))
        o = acc[:, :hd] / (acc[:, hd:hd + 1] + e_sink[g])
        for j in range(SWA_GROUP):
            hq = g * SWA_GROUP + j
            o_ref[:, hq * hd:(hq + 1) * hd] = o[j * blk:(j + 1) * blk].astype(o_ref.dtype)


def _swa(q, kv, kv_meta, sinks, bsz, nq):
    blk = WINDOW
    return pl.pallas_call(
        _swa_kernel,
        grid=(bsz, nq),
        in_specs=[
            pl.BlockSpec((blk, SWA_QD), lambda b, i: (b * nq + i, 0)),
            pl.BlockSpec((blk, 2 * SWA_KVD), lambda b, i: (b * nq + i, 0)),
            pl.BlockSpec((blk, 2 * SWA_KVD), lambda b, i: (b * nq + jnp.maximum(i - 1, 0), 0)),
            pl.BlockSpec((N_META, 2 * SWA_KVD), lambda b, i: (0, 0)),
            pl.BlockSpec((1, SWA_Q_HEADS), lambda b, i: (0, 0)),
        ],
        out_specs=pl.BlockSpec((blk, SWA_QD), lambda b, i: (b * nq + i, 0)),
        out_shape=jax.ShapeDtypeStruct((q.shape[0], SWA_QD), BF16),
        compiler_params=pltpu.CompilerParams(dimension_semantics=("parallel", "parallel"),
                                             vmem_limit_bytes=VMEM_LIMIT),
        name="swa",
    )(q, kv, kv, kv_meta, sinks.reshape(1, -1))


def _merge_kernel(x_ref, eg_ref, eb_ref, og_ref, os_ref, gates_ref, wg_ref, ws_ref, wo_ref, g1_ref, b1_ref,
                  wr_ref, br_ref, h1_ref, idx_ref, tw_ref):
    d = x_ref.shape[1]
    h0 = _layer_norm(x_ref[...], eg_ref[...], eb_ref[...])
    y_g = _dot(og_ref[...], wg_ref[...])
    y_s = _dot(os_ref[...], ws_ref[...])
    mix = _sigmoid(gates_ref[:, :d].astype(F32)) * y_g + _sigmoid(gates_ref[:, d:].astype(F32)) * y_s
    mixed = _dot(mix.astype(BF16), wo_ref[...])
    h1 = _layer_norm(DEEPNORM_ALPHA * h0 + mixed, g1_ref[...], b1_ref[...])
    h1_ref[...] = h1

    h_hi, w_hi = h1.astype(BF16), wr_ref[...].astype(BF16)
    h_lo = (h1 - h_hi.astype(F32)).astype(BF16)
    w_lo = (wr_ref[...] - w_hi.astype(F32)).astype(BF16)
    logits = _dot(h_hi, w_hi) + (_dot(h_hi, w_lo) + _dot(h_lo, w_hi)) + br_ref[...]
    lane = lax.broadcasted_iota(jnp.int32, logits.shape, 1)
    vals, idxs = [], []
    for _ in range(TOP_K):
        m = jnp.max(logits, -1, keepdims=True)
        sel = jnp.min(jnp.where(logits == m, lane, LANES), -1, keepdims=True)
        vals.append(m)
        idxs.append(sel)
        logits = jnp.where(lane == sel, NEG_BIG, logits)
    e = [jnp.exp(v - vals[0]) for v in vals]
    tot = e[0] + e[1] + e[2] + e[3]
    for k in range(TOP_K):
        idx_ref[:, k:k + 1] = idxs[k]
        tw_ref[:, k:k + 1] = e[k] / tot


def _merge(x2d, eg, eb, o_gdn, o_swa, gates, wg, ws, wo, g1, b1, wr, br, tm):
    n, d = x2d.shape
    row = lambda i: (i, 0)
    const = lambda i: (0, 0)
    full = lambda a: pl.BlockSpec(a.shape, const)
    return pl.pallas_call(
        _merge_kernel,
        grid=(n // tm,),
        in_specs=[
            pl.BlockSpec((tm, d), row), full(eg), full(eb),
            pl.BlockSpec((tm, GDN_VD), row), pl.BlockSpec((tm, SWA_QD), row), pl.BlockSpec((tm, 2 * d), row),
            full(wg), full(ws), full(wo), full(g1), full(b1), full(wr), full(br),
        ],
        out_specs=[
            pl.BlockSpec((tm, d), row), pl.BlockSpec((tm, TOP_K), row), pl.BlockSpec((tm, TOP_K), row),
        ],
        out_shape=[
            jax.ShapeDtypeStruct((n, d), F32),
            jax.ShapeDtypeStruct((n, TOP_K), jnp.int32), jax.ShapeDtypeStruct((n, TOP_K), F32),
        ],
        compiler_params=pltpu.CompilerParams(dimension_semantics=("parallel",), vmem_limit_bytes=VMEM_LIMIT),
        name="merge",
    )(x2d, eg, eb, o_gdn, o_swa, gates, wg, ws, wo, g1, b1, wr, br)


def _rank_kernel(idx_ref, rank_ref, counts_ref, run_ref):
    t = idx_ref.shape[0]

    @pl.when(pl.program_id(0) == 0)
    def _():
        run_ref[...] = jnp.zeros_like(run_ref)

    lane = lax.broadcasted_iota(jnp.int32, (t, LANES), 1)
    hits = [idx_ref[:, k:k + 1] == lane for k in range(TOP_K)]
    onehot = hits[0].astype(F32)
    for k in range(1, TOP_K):
        onehot = onehot + hits[k].astype(F32)
    r = lax.broadcasted_iota(jnp.int32, (t, t), 0)
    c = lax.broadcasted_iota(jnp.int32, (t, t), 1)
    before = (r > c).astype(BF16)
    base = run_ref[...] + _dot(before, onehot.astype(BF16))
    for k in range(TOP_K):
        rank_ref[:, k:k + 1] = jnp.sum(jnp.where(hits[k], base, 0.0), -1, keepdims=True).astype(jnp.int32)
    run_ref[...] = run_ref[...] + jnp.sum(onehot, 0, keepdims=True)
    counts_ref[...] = run_ref[...].astype(jnp.int32)


def _rank(top_idx, t):
    n = top_idx.shape[0]
    return pl.pallas_call(
        _rank_kernel,
        grid=(n // t,),
        in_specs=[pl.BlockSpec((t, TOP_K), lambda i: (i, 0))],
        out_specs=[pl.BlockSpec((t, TOP_K), lambda i: (i, 0)), pl.BlockSpec((1, LANES), lambda i: (0, 0))],
        out_shape=[jax.ShapeDtypeStruct((n, TOP_K), jnp.int32), jax.ShapeDtypeStruct((1, LANES), jnp.int32)],
        scratch_shapes=[pltpu.VMEM((1, LANES), F32)],
        compiler_params=pltpu.CompilerParams(dimension_semantics=("arbitrary",)),
        name="rank",
    )(top_idx)


_ROW_GROUP = 8


def _row_copy(src, dst, sem):
    return pltpu.make_async_copy(src, dst, sem)


def _dispatch_kernel(pad_end_ref, dest_hbm, h_ref, xs_hbm, idx_smem, zeros_ref, sem_idx, sem_rows, sem_zero):
    i = pl.program_id(0)
    t = h_ref.shape[0]
    n_idx = t * TOP_K

    @pl.when(i == 0)
    def _():
        zeros_ref[...] = jnp.zeros_like(zeros_ref)
        for e in range(N_EXPERTS):
            end = pad_end_ref[e]
            start = pad_end_ref[e - 1] if e else 0

            @pl.when(end > start)
            def _():
                tile0 = pl.multiple_of(end - MOE_TILE, MOE_TILE)
                fill = _row_copy(zeros_ref, xs_hbm.at[pl.ds(tile0, MOE_TILE)], sem_zero)
                fill.start()
                fill.wait()

    idx_cp = _row_copy(dest_hbm.at[pl.ds(pl.multiple_of(i * n_idx, n_idx), n_idx)], idx_smem, sem_idx)
    idx_cp.start()
    idx_cp.wait()

    def issue(grp, carry):
        row0 = pl.multiple_of(grp * _ROW_GROUP, _ROW_GROUP)
        for r in range(_ROW_GROUP):
            for k in range(TOP_K):
                dst_row = idx_smem[(row0 + r) * TOP_K + k]
                _row_copy(h_ref.at[pl.ds(row0 + r, 1)], xs_hbm.at[pl.ds(dst_row, 1)],
                          sem_rows).start(priority=k % 2)
        return carry

    lax.fori_loop(0, t // _ROW_GROUP, issue, 0)
    _row_copy(xs_hbm.at[pl.ds(0, n_idx)], xs_hbm.at[pl.ds(0, n_idx)], sem_rows).wait()


def _dispatch(pad_end, dest_flat, h1, n_rows, t):
    n, d = h1.shape
    grid_spec = pltpu.PrefetchScalarGridSpec(
        num_scalar_prefetch=1,
        grid=(n // t,),
        in_specs=[
            pl.BlockSpec(memory_space=pl.ANY),
            pl.BlockSpec((t, d), lambda i, pe: (i, 0)),
        ],
        out_specs=pl.BlockSpec(memory_space=pl.ANY),
        scratch_shapes=[
            pltpu.SMEM((t * TOP_K,), jnp.int32),
            pltpu.VMEM((MOE_TILE, d), h1.dtype),
            pltpu.SemaphoreType.DMA, pltpu.SemaphoreType.DMA, pltpu.SemaphoreType.DMA,
        ],
    )
    return pl.pallas_call(
        _dispatch_kernel,
        grid_spec=grid_spec,
        out_shape=jax.ShapeDtypeStruct((n_rows, d), h1.dtype),
        compiler_params=pltpu.CompilerParams(dimension_semantics=("arbitrary",), vmem_limit_bytes=VMEM_LIMIT),
        name="dispatch",
    )(pad_end, dest_flat, h1)


def _expert_kernel(blk_e_ref, n_used_ref, xs_ref, w1_ref, b1_ref, w2_ref, b2_ref, ys_ref):
    del blk_e_ref
    i = pl.program_id(0)
    d_ff = w2_ref.shape[0]

    @pl.when(i < n_used_ref[0])
    def _():
        hdn = _dot(xs_ref[...].astype(BF16), w1_ref[...].astype(BF16)) + b1_ref[...]
        gate = jnp.minimum(hdn[:, :d_ff], SWIGLU_LIMIT)
        up = jnp.clip(hdn[:, d_ff:], -SWIGLU_LIMIT, SWIGLU_LIMIT)
        act = gate * _sigmoid(SWIGLU_ALPHA * gate) * (up + 1.0)
        ys_ref[...] = _dot(act.astype(BF16), w2_ref[...].astype(BF16)) + b2_ref[...]

    @pl.when(i >= n_used_ref[0])
    def _():
        ys_ref[...] = jnp.zeros_like(ys_ref)


def _experts(blk_e, n_used, xs, w1, b1, w2, b2):
    n_rows, w = xs.shape
    n_e, d, d_ff2 = w1.shape
    d_ff = w2.shape[1]
    grid_spec = pltpu.PrefetchScalarGridSpec(
        num_scalar_prefetch=2,
        grid=(n_rows // MOE_TILE,),
        in_specs=[
            pl.BlockSpec((MOE_TILE, w), lambda i, be, nu: (jnp.minimum(i, nu[0] - 1), 0)),
            pl.BlockSpec((None, d, d_ff2), lambda i, be, nu: (be[i], 0, 0)),
            pl.BlockSpec((None, 1, d_ff2), lambda i, be, nu: (be[i], 0, 0)),
            pl.BlockSpec((None, d_ff, d), lambda i, be, nu: (be[i], 0, 0)),
            pl.BlockSpec((None, 1, d), lambda i, be, nu: (be[i], 0, 0)),
        ],
        out_specs=pl.BlockSpec((MOE_TILE, w), lambda i, be, nu: (i, 0)),
    )
    return pl.pallas_call(
        _expert_kernel,
        grid_spec=grid_spec,
        out_shape=jax.ShapeDtypeStruct((n_rows, w), F32),
        compiler_params=pltpu.CompilerParams(dimension_semantics=("arbitrary",), vmem_limit_bytes=VMEM_LIMIT),
        name="experts",
    )(blk_e, n_used, xs, w1, b1.reshape(n_e, 1, d_ff2), w2, b2.reshape(n_e, 1, d))


def _combine_kernel(dest_hbm, ys_hbm, h1_ref, tw_ref, g_ref, b_ref, out_ref, idx_smem, buf, sem_idx, sem_rows):
    i = pl.program_id(0)
    t = h1_ref.shape[0]
    n_idx = t * TOP_K

    def start_gather(step, slot):
        idx_cp = _row_copy(dest_hbm.at[pl.ds(pl.multiple_of(step * n_idx, n_idx), n_idx)], idx_smem, sem_idx)
        idx_cp.start()
        idx_cp.wait()

        def issue(grp, carry):
            row0 = pl.multiple_of(grp * _ROW_GROUP, _ROW_GROUP)
            for r in range(_ROW_GROUP):
                for k in range(TOP_K):
                    src_row = idx_smem[(row0 + r) * TOP_K + k]
                    _row_copy(ys_hbm.at[pl.ds(src_row, 1)], buf.at[slot, k, pl.ds(row0 + r, 1)],
                              sem_rows.at[slot]).start(priority=k % 2)
            return carry

        lax.fori_loop(0, t // _ROW_GROUP, issue, 0)

    @pl.when(i == 0)
    def _():
        start_gather(0, 0)

    @pl.when(i + 1 < pl.num_programs(0))
    def _():
        start_gather(i + 1, (i + 1) % 2)

    slot = i % 2
    _row_copy(buf.at[slot], buf.at[slot], sem_rows.at[slot]).wait()

    ffn = None
    for k in range(TOP_K):
        term = buf[slot, k] * tw_ref[:, k:k + 1]
        ffn = term if ffn is None else ffn + term
    out_ref[...] = _layer_norm(DEEPNORM_ALPHA * h1_ref[...] + ffn, g_ref[...], b_ref[...])


def _combine(dest_flat, ys, h1, top_w, g2, b2, t):
    n, d = h1.shape
    w = ys.shape[1]
    return pl.pallas_call(
        _combine_kernel,
        grid=(n // t,),
        in_specs=[
            pl.BlockSpec(memory_space=pl.ANY),
            pl.BlockSpec(memory_space=pl.ANY),
            pl.BlockSpec((t, d), lambda i: (i, 0)),
            pl.BlockSpec((t, TOP_K), lambda i: (i, 0)),
            pl.BlockSpec((1, d), lambda i: (0, 0)),
            pl.BlockSpec((1, d), lambda i: (0, 0)),
        ],
        out_specs=pl.BlockSpec((t, d), lambda i: (i, 0)),
        out_shape=jax.ShapeDtypeStruct((n, d), F32),
        scratch_shapes=[
            pltpu.SMEM((t * TOP_K,), jnp.int32),
            pltpu.VMEM((2, TOP_K, t, w), ys.dtype),
            pltpu.SemaphoreType.DMA,
            pltpu.SemaphoreType.DMA((2,)),
        ],
        compiler_params=pltpu.CompilerParams(dimension_semantics=("arbitrary",), vmem_limit_bytes=VMEM_LIMIT),
        name="combine",
    )(dest_flat, ys, h1, top_w, g2, b2)


def _largest_tile(n, cap):
    t = cap
    while n % t:
        t //= 2
    return t


def kernel(x, meta_tokens, emb_ln_g, emb_ln_b, w_in, conv_w, a_log, dt_bias, gdn_norm_w, attn_sinks, w_br_gdn,
           w_br_swa, w_out, ln1_g, ln1_b, w_router, b_router, w_moe1, b_moe1, w_moe2, b_moe2, ln2_g, ln2_b):
    bsz, seq, d = x.shape
    assert seq % WINDOW == 0 and seq % CHUNK == 0 and d == GDN_VD
    n = bsz * seq
    x2d = x.reshape(n, d)
    row1 = lambda v: v.reshape(1, -1)
    l = 0

    sp = np.cumsum((GDN_QK, GDN_QK, GDN_VD, GDN_VD, GDN_HEADS, GDN_HEADS, SWA_QD, SWA_KVD, SWA_KVD, d, d))
    w = w_in[l]
    w_perm = jnp.concatenate([
        w[:, :sp[3]], w[:, sp[5]:sp[10]], w[:, sp[3]:sp[5]],
        jnp.zeros((d, LANES - 2 * GDN_HEADS), w.dtype)], axis=1).astype(BF16)
    eg, eb = row1(emb_ln_g), row1(emb_ln_b)

    tm = _largest_tile(n, 512)
    qkv_g, z_g, q_s, kv_s, gates, ba = _in_proj(x2d, eg, eb, w_perm, tm)
    m_qkv, _, _, m_kv, _, m_ba = _in_proj(meta_tokens.astype(x.dtype), eg, eb, w_perm, N_META)

    n_pad = CHUNK - N_META
    n_chunks = seq // CHUNK
    conv_w_t = conv_w[l].T
    pad_rows = lambda a: jnp.pad(a, ((n_pad, 0), (0, 0)))
    chunk_t = lambda a, nb: a[:, GDN_HEADS:2 * GDN_HEADS].reshape(nb, CHUNK, GDN_HEADS).transpose(0, 2, 1)
    m_ba_p = pad_rows(m_ba)
    zeros_state = jnp.zeros((GDN_HEADS, GDN_DK, GDN_DV), F32)
    s_meta = _gdn(pad_rows(m_qkv), jnp.zeros((CHUNK, GDN_VD), BF16), m_ba_p, chunk_t(m_ba_p, 1),
                  jnp.zeros((_HIST, m_qkv.shape[1]), BF16), zeros_state, conv_w_t, a_log[l], dt_bias[l],
                  gdn_norm_w[l], 1, 1, n_pad, True)
    o_gdn = _gdn(qkv_g, z_g, ba, chunk_t(ba, bsz * n_chunks), m_qkv[N_META - _HIST:], s_meta, conv_w_t,
                 a_log[l], dt_bias[l], gdn_norm_w[l], bsz, n_chunks, 0, False)

    o_swa = _swa(q_s, kv_s, m_kv, attn_sinks[l], bsz, seq // WINDOW)

    wr = jnp.pad(w_router[l], ((0, 0), (0, LANES - N_EXPERTS)))
    br = jnp.pad(row1(b_router[l]), ((0, 0), (0, LANES - N_EXPERTS)), constant_values=NEG_BIG)
    h1, top_idx, top_w = _merge(
        x2d, eg, eb, o_gdn, o_swa, gates, w_br_gdn[l].astype(BF16), w_br_swa[l].astype(BF16),
        w_out[l].astype(BF16), row1(ln1_g[l]), row1(ln1_b[l]), wr, br, tm)

    rank, counts = _rank(top_idx, _largest_tile(n, 512))
    counts = counts[0, :N_EXPERTS]
    padded = (counts + MOE_TILE - 1) // MOE_TILE * MOE_TILE
    pad_end = jnp.cumsum(padded)
    pad_start = pad_end - padded
    dest = (pad_start[top_idx] + rank).reshape(-1)
    n_blk = -(-n * TOP_K // MOE_TILE) + N_EXPERTS
    blk_row0 = jnp.arange(n_blk, dtype=jnp.int32) * MOE_TILE
    blk_e = jnp.minimum(jnp.sum((pad_end[None, :] <= blk_row0[:, None]).astype(jnp.int32), axis=1), N_EXPERTS - 1)
    n_used = (pad_end[-1:] // MOE_TILE).astype(jnp.int32)

    tg = _largest_tile(n, GATHER_TOKENS)
    xs = _dispatch(pad_end.astype(jnp.int32), dest, h1, n_blk * MOE_TILE, tg)
    ys = _experts(blk_e, n_used, xs, w_moe1[l], b_moe1[l], w_moe2[l], b_moe2[l])
    out = _combine(dest, ys, h1, top_w, row1(ln2_g[l]), row1(ln2_b[l]), tg)
    return out.reshape(bsz, seq, d)
```

```python
import functools

import jax
import jax.numpy as jnp
import numpy as np
from jax import lax
from jax.experimental import pallas as pl
from jax.experimental.pallas import tpu as pltpu

N_META = 16
GDN_HEADS = 8
GDN_DK = 128
GDN_DV = 128
CONV_WIDTH = 4
CHUNK = 64
SWA_Q_HEADS = 16
SWA_KV_HEADS = 4
SWA_HEAD_DIM = 64
SWA_GROUP = SWA_Q_HEADS // SWA_KV_HEADS
WINDOW = 128
N_EXPERTS = 32
TOP_K = 4
SWIGLU_ALPHA = 1.702
SWIGLU_LIMIT = 7.0
LN_EPS = 1e-5
RMS_EPS = 1e-6
DEPTH = 1
DEEPNORM_ALPHA = (2.0 * DEPTH) ** 0.25

GDN_QK = GDN_HEADS * GDN_DK
GDN_VD = GDN_HEADS * GDN_DV
SWA_QD = SWA_Q_HEADS * SWA_HEAD_DIM
SWA_KVD = SWA_KV_HEADS * SWA_HEAD_DIM

LANES = 128
MOE_TILE = 512
GATHER_TOKENS = 512
NEG_BIG = -1e30
VMEM_LIMIT = 56 * 1024 * 1024

F32 = jnp.float32
BF16 = jnp.bfloat16
HIGHEST = lax.Precision.HIGHEST


def _layer_norm(x, g, b):
    mu = jnp.mean(x, -1, keepdims=True)
    xc = x - mu
    var = jnp.mean(xc * xc, -1, keepdims=True)
    return xc * lax.rsqrt(var + LN_EPS) * g + b


def _sigmoid(x):
    return 1.0 / (1.0 + jnp.exp(-x))


def _silu(x):
    return x * _sigmoid(x)


def _softplus(x):
    return jnp.maximum(x, 0.0) + jnp.log(1.0 + jnp.exp(-jnp.abs(x)))


def _dot(a, b, **kw):
    return jnp.dot(a, b, preferred_element_type=F32, **kw)


def _dot_nt(a, b):
    return lax.dot_general(a, b, (((1,), (1,)), ((), ())), preferred_element_type=F32)


_C_QKVG = 2 * GDN_QK + GDN_VD
_C_Z = _C_QKVG + GDN_VD
_C_QS = _C_Z + SWA_QD
_C_KVS = _C_QS + 2 * SWA_KVD
_C_GATES = _C_KVS + 2 * GDN_VD
_C_BA = _C_GATES + LANES


_HIST = 8
_CONV_COLS = 512


def _inproj_kernel(x_ref, g_ref, b_ref, w_ref, hist0_ref, convw_ref, qkvg_ref, z_ref, qs_ref, kvs_ref, gates_ref,
                   ba_ref, *rest, tiles_per_seq, emit_tail):
    win_ref = rest[-1]
    tm = x_ref.shape[0]
    h = _layer_norm(x_ref[...], g_ref[...], b_ref[...]).astype(BF16)

    @pl.when(pl.program_id(0) % tiles_per_seq == 0)
    def _():
        win_ref[0:_HIST, :] = hist0_ref[...]

    def proj(out_ref, lo, c, ce, act=None):
        y = _dot(h, w_ref[:, c:ce])
        out_ref[:, c - lo:ce - lo] = (y if act is None else act(y)).astype(out_ref.dtype)

    def conv(c):
        cols = slice(c, c + _CONV_COLS)
        acc = win_ref[_HIST:_HIST + tm, cols] * convw_ref[CONV_WIDTH - 1:CONV_WIDTH, cols]
        for i in range(CONV_WIDTH - 1):
            sh = CONV_WIDTH - 1 - i
            acc = acc + win_ref[_HIST - sh:_HIST - sh + tm, cols] * convw_ref[i:i + 1, cols]
        qkvg_ref[:, cols] = _silu(acc).astype(qkvg_ref.dtype)

    for c in range(0, _C_QKVG, 1024):
        win_ref[_HIST:_HIST + tm, c:c + 1024] = _dot(h, w_ref[:, c:c + 1024])
    if emit_tail:
        rest[0][...] = win_ref[tm:tm + _HIST, :]
    others = [(z_ref, _C_QKVG, _C_Z, _silu), (qs_ref, _C_Z, _C_QS, None), (kvs_ref, _C_QS, _C_KVS, None),
              (gates_ref, _C_KVS, _C_GATES, None), (ba_ref, _C_GATES, _C_BA, None)]
    mm = [(ref, lo, c, min(c + 1024, hi), act) for ref, lo, hi, act in others for c in range(lo, hi, 1024)]
    cv = list(range(0, _C_QKVG, _CONV_COLS))
    for k in range(max(len(mm), len(cv))):
        if k < len(mm):
            proj(*mm[k])
        if k < len(cv):
            conv(cv[k])
    win_ref[0:_HIST, :] = win_ref[tm:tm + _HIST, :]


def _in_proj(x2d, ln_g, ln_b, w_perm, hist0, conv_w_t, tm, tiles_per_seq, emit_tail):
    n, d = x2d.shape
    widths = (_C_QKVG, GDN_VD, SWA_QD, 2 * SWA_KVD, _C_GATES - _C_KVS, LANES)
    dtypes = (BF16, BF16, BF16, BF16, BF16, F32)
    row = lambda i: (i, 0)
    const = lambda i: (0, 0)
    out_specs = [pl.BlockSpec((tm, w), row) for w in widths]
    out_shape = [jax.ShapeDtypeStruct((n, w), dt) for w, dt in zip(widths, dtypes)]
    if emit_tail:
        out_specs.append(pl.BlockSpec((_HIST, _C_QKVG), const))
        out_shape.append(jax.ShapeDtypeStruct((_HIST, _C_QKVG), F32))
    return pl.pallas_call(
        functools.partial(_inproj_kernel, tiles_per_seq=tiles_per_seq, emit_tail=emit_tail),
        grid=(n // tm,),
        in_specs=[
            pl.BlockSpec((tm, d), row),
            pl.BlockSpec((1, d), const),
            pl.BlockSpec((1, d), const),
            pl.BlockSpec((d, _C_BA), const, pipeline_mode=pl.Buffered(1)),
            pl.BlockSpec((_HIST, _C_QKVG), const),
            pl.BlockSpec((CONV_WIDTH, _C_QKVG), const),
        ],
        out_specs=out_specs,
        out_shape=out_shape,
        scratch_shapes=[pltpu.VMEM((_HIST + tm, _C_QKVG), F32)],
        compiler_params=pltpu.CompilerParams(dimension_semantics=("arbitrary",), vmem_limit_bytes=VMEM_LIMIT),
        name="in_proj",
    )(x2d, ln_g, ln_b, w_perm, hist0, conv_w_t)


GDN_CHUNKS_PER_STEP = 4
GDN_CHUNKS_PER_BLOCK = 4


def _gdn_kernel(qkv_ref, z_ref, ba_ref, at_ref, s0_ref, alog_r_ref, dtb_r_ref, alog_c_ref, dtb_c_ref, normw_ref,
                out_ref, s_ref, *, n_pad, emit_state):
    chunks_per_step = qkv_ref.shape[0] // CHUNK
    n_sub = min(GDN_CHUNKS_PER_BLOCK, chunks_per_step)

    @pl.when(pl.program_id(1) == 0)
    def _():
        s_ref[...] = s0_ref[...]

    def block(j, carry):
        _gdn_chunks(j, n_sub, qkv_ref, z_ref, ba_ref, at_ref, alog_r_ref, dtb_r_ref, alog_c_ref, dtb_c_ref,
                    normw_ref, None if emit_state else out_ref, s_ref, n_pad)
        return carry

    if chunks_per_step == n_sub:
        block(0, 0)
    else:
        lax.fori_loop(0, chunks_per_step // n_sub, block, 0)
    if emit_state:
        out_ref[...] = s_ref[...]


def _gdn_chunks(j, n_sub, qkv_ref, z_ref, ba_ref, at_ref, alog_r_ref, dtb_r_ref, alog_c_ref, dtb_c_ref,
                normw_ref, o_ref, s_ref, n_pad):
    c = CHUNK
    heads = range(GDN_HEADS)
    row = lax.broadcasted_iota(jnp.int32, (c, c), 0)
    col = lax.broadcasted_iota(jnp.int32, (c, c), 1)
    incl = row >= col
    strict = row > col
    tri_incl = incl.astype(F32)
    tri_upper = (row <= col).astype(F32)

    rows_of, kn_b, qn_b, decay, beta_c, rhs, qdec, kdt_b, g_tot = [], [], [], [], [], [], [], [], []
    for i in range(n_sub):
        ci = j * n_sub + i
        rows = pl.ds(ci * c, c) if isinstance(ci, int) else pl.ds(pl.multiple_of(ci * c, c), c)
        rows_of.append(rows)
        beta = _sigmoid(ba_ref[rows, 0:GDN_HEADS])
        log_g_c = -jnp.exp(alog_r_ref[...]) * _softplus(ba_ref[rows, GDN_HEADS:2 * GDN_HEADS] + dtb_r_ref[...])
        log_g_r = -jnp.exp(alog_c_ref[...]) * _softplus(at_ref[ci] + dtb_c_ref[...])
        if n_pad:
            valid_c = lax.broadcasted_iota(jnp.int32, (c, GDN_HEADS), 0) >= n_pad
            valid_r = lax.broadcasted_iota(jnp.int32, (GDN_HEADS, c), 1) >= n_pad
            beta = jnp.where(valid_c, beta, 0.0)
            log_g_c = jnp.where(valid_c, log_g_c, 0.0)
            log_g_r = jnp.where(valid_r, log_g_r, 0.0)
        gam_c = _dot(tri_incl, log_g_c, precision=HIGHEST)
        gam_r = _dot(log_g_r, tri_upper, precision=HIGHEST)
        for h in heads:
            qh, kh, vh = (qkv_ref[rows, base + h * GDN_DK:base + (h + 1) * GDN_DK].astype(F32)
                          for base in (0, GDN_QK, 2 * GDN_QK))
            qn = qh * lax.rsqrt(jnp.sum(qh * qh, -1, keepdims=True) + RMS_EPS) * (GDN_DK ** -0.5)
            k_n = kh * lax.rsqrt(jnp.sum(kh * kh, -1, keepdims=True) + RMS_EPS)
            b_c = beta[:, h:h + 1]
            g_c = gam_c[:, h:h + 1]
            g_r = gam_r[h:h + 1, :]
            g_last = g_c[c - 1:c, :]
            e_g = jnp.exp(g_c)
            kn_b.append(k_n.astype(BF16))
            qn_b.append(qn.astype(BF16))
            beta_c.append(b_c)
            decay.append(jnp.where(incl, jnp.exp(jnp.where(incl, g_c - g_r, 0.0)), 0.0))
            rhs.append(jnp.concatenate([vh * b_c, k_n * (b_c * e_g)], axis=-1))
            qdec.append(qn * e_g)
            kdt_b.append((k_n * jnp.exp(g_last - g_c)).T.astype(BF16))
            g_tot.append(jnp.exp(g_last))
    pairs = range(n_sub * GDN_HEADS)
    a_mat = [jnp.where(strict, _dot_nt(kn_b[m], kn_b[m]) * decay[m], 0.0) * beta_c[m] for m in pairs]
    qk_b = [(_dot_nt(qn_b[m], kn_b[m]) * decay[m]).astype(BF16) for m in pairs]

    l_mat = [-a for a in a_mat]
    p_b = [a.astype(BF16) for a in a_mat]
    for _ in range(5):
        p = [_dot(b, b) for b in p_b]
        p_b = [x.astype(BF16) for x in p]
        l_mat = [l + x + _dot(l.astype(BF16), xb) for l, x, xb in zip(l_mat, p, p_b)]
    sol = [r + _dot(l.astype(BF16), r.astype(BF16)) for l, r in zip(l_mat, rhs)]
    wq_b = [jnp.concatenate([sol[m][:, GDN_DV:], qdec[m]], axis=0).astype(BF16) for m in pairs]

    for i in range(n_sub):
        of = lambda h: i * GDN_HEADS + h
        state = [s_ref[h] for h in heads]
        p1 = [_dot(wq_b[of(h)], state[h].astype(BF16)) for h in heads]
        u_b = [(sol[of(h)][:, :GDN_DV] - p1[h][:c]).astype(BF16) for h in heads]
        for h in heads:
            s_ref[h] = state[h] * g_tot[of(h)] + _dot(kdt_b[of(h)], u_b[h])
        if o_ref is not None:
            for h in heads:
                o = p1[h][c:] + _dot(qk_b[of(h)], u_b[h])
                o = o * lax.rsqrt(jnp.mean(o * o, -1, keepdims=True) + RMS_EPS) * normw_ref[...]
                gate = z_ref[rows_of[i], h * GDN_DV:(h + 1) * GDN_DV].astype(F32)
                o_ref[rows_of[i], h * GDN_DV:(h + 1) * GDN_DV] = (o * gate).astype(o_ref.dtype)


def _gdn(qkv, z, ba, a_t, s0, a_log, dt_bias, norm_w, bsz, n_chunks, n_pad, emit_state):
    cps = _largest_tile(n_chunks, GDN_CHUNKS_PER_STEP)
    n_steps = n_chunks // cps
    c = cps * CHUNK
    rows = lambda b, n: (b * n_steps + n, 0)
    c2 = lambda b, n: (0, 0)
    c3 = lambda b, n: (0, 0, 0)
    in_specs = [
        pl.BlockSpec((c, qkv.shape[1]), rows),
        pl.BlockSpec((c, z.shape[1]), rows),
        pl.BlockSpec((c, ba.shape[1]), rows),
        pl.BlockSpec((cps, GDN_HEADS, CHUNK), lambda b, n: (b * n_steps + n, 0, 0)),
        pl.BlockSpec(s0.shape, c3),
        pl.BlockSpec((1, GDN_HEADS), c2),
        pl.BlockSpec((1, GDN_HEADS), c2),
        pl.BlockSpec((GDN_HEADS, 1), c2),
        pl.BlockSpec((GDN_HEADS, 1), c2),
        pl.BlockSpec((1, GDN_DV), c2),
    ]
    if emit_state:
        out_specs = pl.BlockSpec(s0.shape, c3)
        out_shape = jax.ShapeDtypeStruct(s0.shape, F32)
    else:
        out_specs = pl.BlockSpec((c, GDN_VD), rows)
        out_shape = jax.ShapeDtypeStruct((qkv.shape[0], GDN_VD), BF16)
    return pl.pallas_call(
        functools.partial(_gdn_kernel, n_pad=n_pad, emit_state=emit_state),
        grid=(bsz, n_steps),
        in_specs=in_specs,
        out_specs=out_specs,
        out_shape=out_shape,
        scratch_shapes=[pltpu.VMEM((GDN_HEADS, GDN_DK, GDN_DV), F32)],
        compiler_params=pltpu.CompilerParams(dimension_semantics=("parallel", "arbitrary"),
                                             vmem_limit_bytes=VMEM_LIMIT),
        name="gdn_meta" if emit_state else "gdn",
    )(qkv, z, ba, a_t, s0, a_log.reshape(1, -1), dt_bias.reshape(1, -1),
      a_log.reshape(-1, 1), dt_bias.reshape(-1, 1), norm_w.reshape(1, -1))


def _swa_kernel(q_ref, kvc_ref, kvp_ref, kvm_ref, sink_ref, o_ref):
    i = pl.program_id(1)
    blk = WINDOW
    hd = SWA_HEAD_DIM
    rows = SWA_GROUP * blk
    t = lax.broadcasted_iota(jnp.int32, (rows, blk), 0) & (blk - 1)
    s = lax.broadcasted_iota(jnp.int32, (rows, blk), 1)
    d_cur = (t - s).astype(F32)
    m_cur = t >= s
    m_prev = jnp.logical_and(s > t, i > 0)
    tm_ = lax.broadcasted_iota(jnp.int32, (rows, N_META), 0) & (blk - 1)
    sm_ = lax.broadcasted_iota(jnp.int32, (rows, N_META), 1)
    d_meta = (tm_ - sm_ + N_META + i * blk).astype(F32)
    ones = jnp.ones((blk, hd), BF16)
    ones_meta = jnp.ones((N_META, hd), BF16)
    groups = range(SWA_KV_HEADS)

    def per_row(vals):
        return jnp.concatenate([jnp.broadcast_to(v, (blk, 1)) for v in vals], axis=0)

    sc, sinks = [], []
    for g in groups:
        hqs = [g * SWA_GROUP + j for j in range(SWA_GROUP)]
        q_g = jnp.concatenate([q_ref[:, hq * hd:(hq + 1) * hd] for hq in hqs], axis=0) * (hd ** -0.5)
        slope = per_row([jnp.full((1, 1), 2.0 ** (-8.0 * (hq + 1) / SWA_Q_HEADS), F32) for hq in hqs])
        bias_cur = slope * d_cur
        sc_cur = jnp.where(m_cur, _dot_nt(q_g, kvc_ref[:, g * hd:(g + 1) * hd]) - bias_cur, NEG_BIG)
        sc_prev = jnp.where(m_prev, _dot_nt(q_g, kvp_ref[:, g * hd:(g + 1) * hd]) - (bias_cur + slope * blk),
                            NEG_BIG)
        sc_meta = _dot_nt(q_g, kvm_ref[:, g * hd:(g + 1) * hd]) - slope * d_meta
        sc.append((sc_cur, sc_prev, sc_meta))
        sinks.append(per_row([sink_ref[:, hq:hq + 1] for hq in hqs]))
    e, e_sink = [], []
    for g in groups:
        sc_cur, sc_prev, sc_meta = sc[g]
        m = jnp.maximum(jnp.maximum(jnp.max(jnp.maximum(sc_cur, sc_prev), -1, keepdims=True),
                                    jnp.max(sc_meta, -1, keepdims=True)), sinks[g])
        e.append(tuple(jnp.exp(x - m).astype(BF16) for x in sc[g]))
        e_sink.append(jnp.exp(sinks[g] - m))
    for g in groups:
        v_of = lambda ref, pad: jnp.concatenate([ref[:, SWA_KVD + g * hd:SWA_KVD + (g + 1) * hd], pad], axis=-1)
        acc = (_dot(e[g][0], v_of(kvc_ref, ones)) + _dot(e[g][1], v_of(kvp_ref, ones))
               + _dot(e[g][2], v_of(kvm_ref, ones_meta)))
        o = acc[:, :hd] / (acc[:, hd:hd + 1] + e_sink[g])
        for j in range(SWA_GROUP):
            hq = g * SWA_GROUP + j
            o_ref[:, hq * hd:(hq + 1) * hd] = o[j * blk:(j + 1) * blk].astype(o_ref.dtype)


def _swa(q, kv, kv_meta, sinks, bsz, nq):
    blk = WINDOW
    return pl.pallas_call(
        _swa_kernel,
        grid=(bsz, nq),
        in_specs=[
            pl.BlockSpec((blk, SWA_QD), lambda b, i: (b * nq + i, 0)),
            pl.BlockSpec((blk, 2 * SWA_KVD), lambda b, i: (b * nq + i, 0)),
            pl.BlockSpec((blk, 2 * SWA_KVD), lambda b, i: (b * nq + jnp.maximum(i - 1, 0), 0)),
            pl.BlockSpec((N_META, 2 * SWA_KVD), lambda b, i: (0, 0)),
            pl.BlockSpec((1, SWA_Q_HEADS), lambda b, i: (0, 0)),
        ],
        out_specs=pl.BlockSpec((blk, SWA_QD), lambda b, i: (b * nq + i, 0)),
        out_shape=jax.ShapeDtypeStruct((q.shape[0], SWA_QD), BF16),
        compiler_params=pltpu.CompilerParams(dimension_semantics=("parallel", "parallel"),
                                             vmem_limit_bytes=VMEM_LIMIT),
        name="swa",
    )(q, kv, kv, kv_meta, sinks.reshape(1, -1))


def _merge_kernel(x_ref, eg_ref, eb_ref, og_ref, os_ref, gates_ref, wg_ref, ws_ref, wo_ref, g1_ref, b1_ref,
                  wr_ref, br_ref, h1_ref, idx_ref, tw_ref):
    d = x_ref.shape[1]
    h0 = _layer_norm(x_ref[...], eg_ref[...], eb_ref[...])
    y_g = _dot(og_ref[...], wg_ref[...])
    y_s = _dot(os_ref[...], ws_ref[...])
    mix = _sigmoid(gates_ref[:, :d].astype(F32)) * y_g + _sigmoid(gates_ref[:, d:].astype(F32)) * y_s
    mixed = _dot(mix.astype(BF16), wo_ref[...])
    h1 = _layer_norm(DEEPNORM_ALPHA * h0 + mixed, g1_ref[...], b1_ref[...])
    h1_ref[...] = h1

    h_hi, w_hi = h1.astype(BF16), wr_ref[...].astype(BF16)
    h_lo = (h1 - h_hi.astype(F32)).astype(BF16)
    w_lo = (wr_ref[...] - w_hi.astype(F32)).astype(BF16)
    logits = _dot(h_hi, w_hi) + (_dot(h_hi, w_lo) + _dot(h_lo, w_hi)) + br_ref[...]
    lane = lax.broadcasted_iota(jnp.int32, logits.shape, 1)
    vals, idxs = [], []
    for _ in range(TOP_K):
        m = jnp.max(logits, -1, keepdims=True)
        sel = jnp.min(jnp.where(logits == m, lane, LANES), -1, keepdims=True)
        vals.append(m)
        idxs.append(sel)
        logits = jnp.where(lane == sel, NEG_BIG, logits)
    e = [jnp.exp(v - vals[0]) for v in vals]
    tot = e[0] + e[1] + e[2] + e[3]
    for k in range(TOP_K):
        idx_ref[:, k:k + 1] = idxs[k]
        tw_ref[:, k:k + 1] = e[k] / tot


def _merge(x2d, eg, eb, o_gdn, o_swa, gates, wg, ws, wo, g1, b1, wr, br, tm):
    n, d = x2d.shape
    row = lambda i: (i, 0)
    const = lambda i: (0, 0)
    full = lambda a: pl.BlockSpec(a.shape, const)
    return pl.pallas_call(
        _merge_kernel,
        grid=(n // tm,),
        in_specs=[
            pl.BlockSpec((tm, d), row), full(eg), full(eb),
            pl.BlockSpec((tm, GDN_VD), row), pl.BlockSpec((tm, SWA_QD), row), pl.BlockSpec((tm, 2 * d), row),
            full(wg), full(ws), full(wo), full(g1), full(b1), full(wr), full(br),
        ],
        out_specs=[
            pl.BlockSpec((tm, d), row), pl.BlockSpec((tm, TOP_K), row), pl.BlockSpec((tm, TOP_K), row),
        ],
        out_shape=[
            jax.ShapeDtypeStruct((n, d), F32),
            jax.ShapeDtypeStruct((n, TOP_K), jnp.int32), jax.ShapeDtypeStruct((n, TOP_K), F32),
        ],
        compiler_params=pltpu.CompilerParams(dimension_semantics=("parallel",), vmem_limit_bytes=VMEM_LIMIT),
        name="merge",
    )(x2d, eg, eb, o_gdn, o_swa, gates, wg, ws, wo, g1, b1, wr, br)


def _rank_kernel(idx_ref, rank_ref, counts_ref, run_ref):
    t = idx_ref.shape[0]

    @pl.when(pl.program_id(0) == 0)
    def _():
        run_ref[...] = jnp.zeros_like(run_ref)

    lane = lax.broadcasted_iota(jnp.int32, (t, LANES), 1)
    hits = [idx_ref[:, k:k + 1] == lane for k in range(TOP_K)]
    onehot = hits[0].astype(F32)
    for k in range(1, TOP_K):
        onehot = onehot + hits[k].astype(F32)
    r = lax.broadcasted_iota(jnp.int32, (t, t), 0)
    c = lax.broadcasted_iota(jnp.int32, (t, t), 1)
    before = (r > c).astype(BF16)
    base = run_ref[...] + _dot(before, onehot.astype(BF16))
    for k in range(TOP_K):
        rank_ref[:, k:k + 1] = jnp.sum(jnp.where(hits[k], base, 0.0), -1, keepdims=True).astype(jnp.int32)
    run_ref[...] = run_ref[...] + jnp.sum(onehot, 0, keepdims=True)
    counts_ref[...] = run_ref[...].astype(jnp.int32)


def _rank(top_idx, t):
    n = top_idx.shape[0]
    return pl.pallas_call(
        _rank_kernel,
        grid=(n // t,),
        in_specs=[pl.BlockSpec((t, TOP_K), lambda i: (i, 0))],
        out_specs=[pl.BlockSpec((t, TOP_K), lambda i: (i, 0)), pl.BlockSpec((1, LANES), lambda i: (0, 0))],
        out_shape=[jax.ShapeDtypeStruct((n, TOP_K), jnp.int32), jax.ShapeDtypeStruct((1, LANES), jnp.int32)],
        scratch_shapes=[pltpu.VMEM((1, LANES), F32)],
        compiler_params=pltpu.CompilerParams(dimension_semantics=("arbitrary",)),
        name="rank",
    )(top_idx)


_ROW_GROUP = 8


def _row_copy(src, dst, sem):
    return pltpu.make_async_copy(src, dst, sem)


def _dispatch_kernel(pad_end_ref, dest_hbm, h_ref, xs_hbm, idx_smem, zeros_ref, sem_idx, sem_rows, sem_zero):
    i = pl.program_id(0)
    t = h_ref.shape[0]
    n_idx = t * TOP_K

    @pl.when(i == 0)
    def _():
        zeros_ref[...] = jnp.zeros_like(zeros_ref)
        for e in range(N_EXPERTS):
            end = pad_end_ref[e]
            start = pad_end_ref[e - 1] if e else 0

            @pl.when(end > start)
            def _():
                tile0 = pl.multiple_of(end - MOE_TILE, MOE_TILE)
                fill = _row_copy(zeros_ref, xs_hbm.at[pl.ds(tile0, MOE_TILE)], sem_zero)
                fill.start()
                fill.wait()

    idx_cp = _row_copy(dest_hbm.at[pl.ds(pl.multiple_of(i * n_idx, n_idx), n_idx)], idx_smem, sem_idx)
    idx_cp.start()
    idx_cp.wait()

    def issue(grp, carry):
        row0 = pl.multiple_of(grp * _ROW_GROUP, _ROW_GROUP)
        for r in range(_ROW_GROUP):
            for k in range(TOP_K):
                dst_row = idx_smem[(row0 + r) * TOP_K + k]
                _row_copy(h_ref.at[pl.ds(row0 + r, 1)], xs_hbm.at[pl.ds(dst_row, 1)],
                          sem_rows).start(priority=k % 2)
        return carry

    lax.fori_loop(0, t // _ROW_GROUP, issue, 0)
    _row_copy(xs_hbm.at[pl.ds(0, n_idx)], xs_hbm.at[pl.ds(0, n_idx)], sem_rows).wait()


def _dispatch(pad_end, dest_flat, h1, n_rows, t):
    n, d = h1.shape
    grid_spec = pltpu.PrefetchScalarGridSpec(
        num_scalar_prefetch=1,
        grid=(n // t,),
        in_specs=[
            pl.BlockSpec(memory_space=pl.ANY),
            pl.BlockSpec((t, d), lambda i, pe: (i, 0)),
        ],
        out_specs=pl.BlockSpec(memory_space=pl.ANY),
        scratch_shapes=[
            pltpu.SMEM((t * TOP_K,), jnp.int32),
            pltpu.VMEM((MOE_TILE, d), h1.dtype),
            pltpu.SemaphoreType.DMA, pltpu.SemaphoreType.DMA, pltpu.SemaphoreType.DMA,
        ],
    )
    return pl.pallas_call(
        _dispatch_kernel,
        grid_spec=grid_spec,
        out_shape=jax.ShapeDtypeStruct((n_rows, d), h1.dtype),
        compiler_params=pltpu.CompilerParams(dimension_semantics=("arbitrary",), vmem_limit_bytes=VMEM_LIMIT),
        name="dispatch",
    )(pad_end, dest_flat, h1)


def _expert_kernel(blk_e_ref, n_used_ref, xs_ref, w1_ref, b1_ref, w2_ref, b2_ref, ys_ref):
    del blk_e_ref
    i = pl.program_id(0)
    d_ff = w2_ref.shape[0]

    @pl.when(i < n_used_ref[0])
    def _():
        hdn = _dot(xs_ref[...].astype(BF16), w1_ref[...].astype(BF16)) + b1_ref[...]
        gate = jnp.minimum(hdn[:, :d_ff], SWIGLU_LIMIT)
        up = jnp.clip(hdn[:, d_ff:], -SWIGLU_LIMIT, SWIGLU_LIMIT)
        act = gate * _sigmoid(SWIGLU_ALPHA * gate) * (up + 1.0)
        ys_ref[...] = _dot(act.astype(BF16), w2_ref[...].astype(BF16)) + b2_ref[...]

    @pl.when(i >= n_used_ref[0])
    def _():
        ys_ref[...] = jnp.zeros_like(ys_ref)


def _experts(blk_e, n_used, xs, w1, b1, w2, b2):
    n_rows, w = xs.shape
    n_e, d, d_ff2 = w1.shape
    d_ff = w2.shape[1]
    grid_spec = pltpu.PrefetchScalarGridSpec(
        num_scalar_prefetch=2,
        grid=(n_rows // MOE_TILE,),
        in_specs=[
            pl.BlockSpec((MOE_TILE, w), lambda i, be, nu: (jnp.minimum(i, nu[0] - 1), 0)),
            pl.BlockSpec((None, d, d_ff2), lambda i, be, nu: (be[i], 0, 0)),
            pl.BlockSpec((None, 1, d_ff2), lambda i, be, nu: (be[i], 0, 0)),
            pl.BlockSpec((None, d_ff, d), lambda i, be, nu: (be[i], 0, 0)),
            pl.BlockSpec((None, 1, d), lambda i, be, nu: (be[i], 0, 0)),
        ],
        out_specs=pl.BlockSpec((MOE_TILE, w), lambda i, be, nu: (i, 0)),
    )
    return pl.pallas_call(
        _expert_kernel,
        grid_spec=grid_spec,
        out_shape=jax.ShapeDtypeStruct((n_rows, w), F32),
        compiler_params=pltpu.CompilerParams(dimension_semantics=("arbitrary",), vmem_limit_bytes=VMEM_LIMIT),
        name="experts",
    )(blk_e, n_used, xs, w1, b1.reshape(n_e, 1, d_ff2), w2, b2.reshape(n_e, 1, d))


def _combine_kernel(dest_hbm, ys_hbm, h1_ref, tw_ref, g_ref, b_ref, out_ref, idx_smem, buf, sem_idx, sem_rows):
    i = pl.program_id(0)
    t = h1_ref.shape[0]
    n_idx = t * TOP_K

    def start_gather(step, slot):
        idx_cp = _row_copy(dest_hbm.at[pl.ds(pl.multiple_of(step * n_idx, n_idx), n_idx)], idx_smem, sem_idx)
        idx_cp.start()
        idx_cp.wait()

        def issue(grp, carry):
            row0 = pl.multiple_of(grp * _ROW_GROUP, _ROW_GROUP)
            for r in range(_ROW_GROUP):
                for k in range(TOP_K):
                    src_row = idx_smem[(row0 + r) * TOP_K + k]
                    _row_copy(ys_hbm.at[pl.ds(src_row, 1)], buf.at[slot, k, pl.ds(row0 + r, 1)],
                              sem_rows.at[slot]).start(priority=k % 2)
            return carry

        lax.fori_loop(0, t // _ROW_GROUP, issue, 0)

    @pl.when(i == 0)
    def _():
        start_gather(0, 0)

    @pl.when(i + 1 < pl.num_programs(0))
    def _():
        start_gather(i + 1, (i + 1) % 2)

    slot = i % 2
    _row_copy(buf.at[slot], buf.at[slot], sem_rows.at[slot]).wait()

    ffn = None
    for k in range(TOP_K):
        term = buf[slot, k] * tw_ref[:, k:k + 1]
        ffn = term if ffn is None else ffn + term
    out_ref[...] = _layer_norm(DEEPNORM_ALPHA * h1_ref[...] + ffn, g_ref[...], b_ref[...])


def _combine(dest_flat, ys, h1, top_w, g2, b2, t):
    n, d = h1.shape
    w = ys.shape[1]
    return pl.pallas_call(
        _combine_kernel,
        grid=(n // t,),
        in_specs=[
            pl.BlockSpec(memory_space=pl.ANY),
            pl.BlockSpec(memory_space=pl.ANY),
            pl.BlockSpec((t, d), lambda i: (i, 0)),
            pl.BlockSpec((t, TOP_K), lambda i: (i, 0)),
            pl.BlockSpec((1, d), lambda i: (0, 0)),
            pl.BlockSpec((1, d), lambda i: (0, 0)),
        ],
        out_specs=pl.BlockSpec((t, d), lambda i: (i, 0)),
        out_shape=jax.ShapeDtypeStruct((n, d), F32),
        scratch_shapes=[
            pltpu.SMEM((t * TOP_K,), jnp.int32),
            pltpu.VMEM((2, TOP_K, t, w), ys.dtype),
            pltpu.SemaphoreType.DMA,
            pltpu.SemaphoreType.DMA((2,)),
        ],
        compiler_params=pltpu.CompilerParams(dimension_semantics=("arbitrary",), vmem_limit_bytes=VMEM_LIMIT),
        name="combine",
    )(dest_flat, ys, h1, top_w, g2, b2)


def _largest_tile(n, cap):
    t = cap
    while n % t:
        t //= 2
    return t


def kernel(x, meta_tokens, emb_ln_g, emb_ln_b, w_in, conv_w, a_log, dt_bias, gdn_norm_w, attn_sinks, w_br_gdn,
           w_br_swa, w_out, ln1_g, ln1_b, w_router, b_router, w_moe1, b_moe1, w_moe2, b_moe2, ln2_g, ln2_b):
    bsz, seq, d = x.shape
    assert seq % WINDOW == 0 and seq % CHUNK == 0 and d == GDN_VD
    n = bsz * seq
    x2d = x.reshape(n, d)
    row1 = lambda v: v.reshape(1, -1)
    l = 0

    sp = np.cumsum((GDN_QK, GDN_QK, GDN_VD, GDN_VD, GDN_HEADS, GDN_HEADS, SWA_QD, SWA_KVD, SWA_KVD, d, d))
    w = w_in[l]
    w_perm = jnp.concatenate([
        w[:, :sp[3]], w[:, sp[5]:sp[10]], w[:, sp[3]:sp[5]],
        jnp.zeros((d, LANES - 2 * GDN_HEADS), w.dtype)], axis=1).astype(BF16)
    eg, eb = row1(emb_ln_g), row1(emb_ln_b)

    tm = _largest_tile(seq, 512)
    conv_w_t = conv_w[l].T
    m_qkv, _, _, m_kv, _, m_ba, m_tail = _in_proj(
        meta_tokens.astype(x.dtype), eg, eb, w_perm, jnp.zeros((_HIST, _C_QKVG), F32), conv_w_t, N_META, 1, True)
    qkv_g, z_g, q_s, kv_s, gates, ba = _in_proj(x2d, eg, eb, w_perm, m_tail, conv_w_t, tm, seq // tm, False)

    n_pad = CHUNK - N_META
    n_chunks = seq // CHUNK
    pad_rows = lambda a: jnp.pad(a, ((n_pad, 0), (0, 0)))
    chunk_t = lambda a, nb: a[:, GDN_HEADS:2 * GDN_HEADS].reshape(nb, CHUNK, GDN_HEADS).transpose(0, 2, 1)
    m_ba_p = pad_rows(m_ba)
    zeros_state = jnp.zeros((GDN_HEADS, GDN_DK, GDN_DV), F32)
    s_meta = _gdn(pad_rows(m_qkv), jnp.zeros((CHUNK, GDN_VD), BF16), m_ba_p, chunk_t(m_ba_p, 1), zeros_state,
                  a_log[l], dt_bias[l], gdn_norm_w[l], 1, 1, n_pad, True)
    o_gdn = _gdn(qkv_g, z_g, ba, chunk_t(ba, bsz * n_chunks), s_meta, a_log[l], dt_bias[l], gdn_norm_w[l],
                 bsz, n_chunks, 0, False)

    o_swa = _swa(q_s, kv_s, m_kv, attn_sinks[l], bsz, seq // WINDOW)

    wr = jnp.pad(w_router[l], ((0, 0), (0, LANES - N_EXPERTS)))
    br = jnp.pad(row1(b_router[l]), ((0, 0), (0, LANES - N_EXPERTS)), constant_values=NEG_BIG)
    h1, top_idx, top_w = _merge(
        x2d, eg, eb, o_gdn, o_swa, gates, w_br_gdn[l].astype(BF16), w_br_swa[l].astype(BF16),
        w_out[l].astype(BF16), row1(ln1_g[l]), row1(ln1_b[l]), wr, br, tm)

    rank, counts = _rank(top_idx, _largest_tile(n, 512))
    counts = counts[0, :N_EXPERTS]
    padded = (counts + MOE_TILE - 1) // MOE_TILE * MOE_TILE
    pad_end = jnp.cumsum(padded)
    pad_start = pad_end - padded
    dest = (pad_start[top_idx] + rank).reshape(-1)
    n_blk = -(-n * TOP_K // MOE_TILE) + N_EXPERTS
    blk_row0 = jnp.arange(n_blk, dtype=jnp.int32) * MOE_TILE
    blk_e = jnp.minimum(jnp.sum((pad_end[None, :] <= blk_row0[:, None]).astype(jnp.int32), axis=1), N_EXPERTS - 1)
    n_used = (pad_end[-1:] // MOE_TILE).astype(jnp.int32)

    tg = _largest_tile(n, GATHER_TOKENS)
    xs = _dispatch(pad_end.astype(jnp.int32), dest, h1, n_blk * MOE_TILE, tg)
    ys = _experts(blk_e, n_used, xs, w_moe1[l], b_moe1[l], w_moe2[l], b_moe2[l])
    out = _combine(dest, ys, h1, top_w, row1(ln2_g[l]), row1(ln2_b[l]), tg)
    return out.reshape(bsz, seq, d)
```

```python
import functools

import jax
import jax.numpy as jnp
import numpy as np
from jax import lax
from jax.experimental import pallas as pl
from jax.experimental.pallas import tpu as pltpu

N_META = 16
GDN_HEADS = 8
GDN_DK = 128
GDN_DV = 128
CONV_WIDTH = 4
CHUNK = 64
SWA_Q_HEADS = 16
SWA_KV_HEADS = 4
SWA_HEAD_DIM = 64
SWA_GROUP = SWA_Q_HEADS // SWA_KV_HEADS
WINDOW = 128
N_EXPERTS = 32
TOP_K = 4
SWIGLU_ALPHA = 1.702
SWIGLU_LIMIT = 7.0
LN_EPS = 1e-5
RMS_EPS = 1e-6
DEPTH = 1
DEEPNORM_ALPHA = (2.0 * DEPTH) ** 0.25

GDN_QK = GDN_HEADS * GDN_DK
GDN_VD = GDN_HEADS * GDN_DV
SWA_QD = SWA_Q_HEADS * SWA_HEAD_DIM
SWA_KVD = SWA_KV_HEADS * SWA_HEAD_DIM

LANES = 128
MOE_TILE = 512
GATHER_TOKENS = 512
NEG_BIG = -1e30
VMEM_LIMIT = 56 * 1024 * 1024

F32 = jnp.float32
BF16 = jnp.bfloat16
HIGHEST = lax.Precision.HIGHEST


def _layer_norm(x, g, b):
    mu = jnp.mean(x, -1, keepdims=True)
    xc = x - mu
    var = jnp.mean(xc * xc, -1, keepdims=True)
    return xc * lax.rsqrt(var + LN_EPS) * g + b


def _sigmoid(x):
    return 1.0 / (1.0 + jnp.exp(-x))


def _silu(x):
    return x * _sigmoid(x)


def _softplus(x):
    return jnp.maximum(x, 0.0) + jnp.log(1.0 + jnp.exp(-jnp.abs(x)))


def _dot(a, b, **kw):
    return jnp.dot(a, b, preferred_element_type=F32, **kw)


_SUB = 8


def _load_token_tiles(ref, t, lead=()):
    return jnp.concatenate([ref[(*lead, pl.ds(j, t, stride=_SUB), slice(None))] for j in range(_SUB)], axis=-1)


def _store_token_tiles(ref, x):
    for j in range(_SUB):
        ref[pl.ds(j, x.shape[0], stride=_SUB), :] = x[:, j * LANES:(j + 1) * LANES]


def _dot_nt(a, b):
    return lax.dot_general(a, b, (((1,), (1,)), ((), ())), preferred_element_type=F32)


_C_QKVG = 2 * GDN_QK + GDN_VD
_C_Z = _C_QKVG + GDN_VD
_C_QS = _C_Z + SWA_QD
_C_KVS = _C_QS + 2 * SWA_KVD
_C_GATES = _C_KVS + 2 * GDN_VD
_C_BA = _C_GATES + LANES


_HIST = 8
_CONV_COLS = 512


def _inproj_kernel(x_ref, g_ref, b_ref, w_ref, hist0_ref, convw_ref, qkvg_ref, z_ref, qs_ref, kvs_ref, gates_ref,
                   ba_ref, *rest, tiles_per_seq, emit_tail):
    win_ref = rest[-1]
    tm = x_ref.shape[0]
    h = _layer_norm(x_ref[...], g_ref[...], b_ref[...]).astype(BF16)

    @pl.when(pl.program_id(0) % tiles_per_seq == 0)
    def _():
        win_ref[0:_HIST, :] = hist0_ref[...]

    def proj(out_ref, lo, c, ce, act=None):
        y = _dot(h, w_ref[:, c:ce])
        out_ref[:, c - lo:ce - lo] = (y if act is None else act(y)).astype(out_ref.dtype)

    def conv(c):
        cols = slice(c, c + _CONV_COLS)
        acc = win_ref[_HIST:_HIST + tm, cols] * convw_ref[CONV_WIDTH - 1:CONV_WIDTH, cols]
        for i in range(CONV_WIDTH - 1):
            sh = CONV_WIDTH - 1 - i
            acc = acc + win_ref[_HIST - sh:_HIST - sh + tm, cols] * convw_ref[i:i + 1, cols]
        qkvg_ref[:, cols] = _silu(acc).astype(qkvg_ref.dtype)

    for c in range(0, _C_QKVG, 1024):
        win_ref[_HIST:_HIST + tm, c:c + 1024] = _dot(h, w_ref[:, c:c + 1024])
    if emit_tail:
        rest[0][...] = win_ref[tm:tm + _HIST, :]
    others = [(z_ref, _C_QKVG, _C_Z, _silu), (qs_ref, _C_Z, _C_QS, None), (kvs_ref, _C_QS, _C_KVS, None),
              (gates_ref, _C_KVS, _C_GATES, None), (ba_ref, _C_GATES, _C_BA, None)]
    mm = [(ref, lo, c, min(c + 1024, hi), act) for ref, lo, hi, act in others for c in range(lo, hi, 1024)]
    cv = list(range(0, _C_QKVG, _CONV_COLS))
    for k in range(max(len(mm), len(cv))):
        if k < len(mm):
            proj(*mm[k])
        if k < len(cv):
            conv(cv[k])
    win_ref[0:_HIST, :] = win_ref[tm:tm + _HIST, :]


def _in_proj(x2d, ln_g, ln_b, w_perm, hist0, conv_w_t, tm, tiles_per_seq, emit_tail):
    n, d = x2d.shape
    widths = (_C_QKVG, GDN_VD, SWA_QD, 2 * SWA_KVD, _C_GATES - _C_KVS, LANES)
    dtypes = (BF16, BF16, BF16, BF16, BF16, F32)
    row = lambda i: (i, 0)
    const = lambda i: (0, 0)
    out_specs = [pl.BlockSpec((tm, w), row) for w in widths]
    out_shape = [jax.ShapeDtypeStruct((n, w), dt) for w, dt in zip(widths, dtypes)]
    if emit_tail:
        out_specs.append(pl.BlockSpec((_HIST, _C_QKVG), const))
        out_shape.append(jax.ShapeDtypeStruct((_HIST, _C_QKVG), F32))
    return pl.pallas_call(
        functools.partial(_inproj_kernel, tiles_per_seq=tiles_per_seq, emit_tail=emit_tail),
        grid=(n // tm,),
        in_specs=[
            pl.BlockSpec((tm, d), row),
            pl.BlockSpec((1, d), const),
            pl.BlockSpec((1, d), const),
            pl.BlockSpec((d, _C_BA), const, pipeline_mode=pl.Buffered(1)),
            pl.BlockSpec((_HIST, _C_QKVG), const),
            pl.BlockSpec((CONV_WIDTH, _C_QKVG), const),
        ],
        out_specs=out_specs,
        out_shape=out_shape,
        scratch_shapes=[pltpu.VMEM((_HIST + tm, _C_QKVG), F32)],
        compiler_params=pltpu.CompilerParams(dimension_semantics=("arbitrary",), vmem_limit_bytes=VMEM_LIMIT),
        name="in_proj",
    )(x2d, ln_g, ln_b, w_perm, hist0, conv_w_t)


GDN_CHUNKS_PER_STEP = 4
GDN_CHUNKS_PER_BLOCK = 4


def _gdn_kernel(qkv_ref, z_ref, ba_ref, at_ref, s0_ref, alog_r_ref, dtb_r_ref, alog_c_ref, dtb_c_ref, normw_ref,
                out_ref, s_ref, *, n_pad, emit_state):
    chunks_per_step = qkv_ref.shape[0] // CHUNK
    n_sub = min(GDN_CHUNKS_PER_BLOCK, chunks_per_step)

    @pl.when(pl.program_id(1) == 0)
    def _():
        s_ref[...] = s0_ref[...]

    def block(j, carry):
        _gdn_chunks(j, n_sub, qkv_ref, z_ref, ba_ref, at_ref, alog_r_ref, dtb_r_ref, alog_c_ref, dtb_c_ref,
                    normw_ref, None if emit_state else out_ref, s_ref, n_pad)
        return carry

    if chunks_per_step == n_sub:
        block(0, 0)
    else:
        lax.fori_loop(0, chunks_per_step // n_sub, block, 0)
    if emit_state:
        out_ref[...] = s_ref[...]


def _gdn_chunks(j, n_sub, qkv_ref, z_ref, ba_ref, at_ref, alog_r_ref, dtb_r_ref, alog_c_ref, dtb_c_ref,
                normw_ref, o_ref, s_ref, n_pad):
    c = CHUNK
    heads = range(GDN_HEADS)
    row = lax.broadcasted_iota(jnp.int32, (c, c), 0)
    col = lax.broadcasted_iota(jnp.int32, (c, c), 1)
    incl = row >= col
    strict = row > col
    tri_incl = incl.astype(F32)
    tri_upper = (row <= col).astype(F32)

    rows_of, kn_b, qn_b, decay, beta_c, rhs, qdec, kdt_b, g_tot = [], [], [], [], [], [], [], [], []
    for i in range(n_sub):
        ci = j * n_sub + i
        rows = pl.ds(ci * c, c) if isinstance(ci, int) else pl.ds(pl.multiple_of(ci * c, c), c)
        rows_of.append(rows)
        beta = _sigmoid(ba_ref[rows, 0:GDN_HEADS])
        log_g_c = -jnp.exp(alog_r_ref[...]) * _softplus(ba_ref[rows, GDN_HEADS:2 * GDN_HEADS] + dtb_r_ref[...])
        log_g_r = -jnp.exp(alog_c_ref[...]) * _softplus(at_ref[ci] + dtb_c_ref[...])
        if n_pad:
            valid_c = lax.broadcasted_iota(jnp.int32, (c, GDN_HEADS), 0) >= n_pad
            valid_r = lax.broadcasted_iota(jnp.int32, (GDN_HEADS, c), 1) >= n_pad
            beta = jnp.where(valid_c, beta, 0.0)
            log_g_c = jnp.where(valid_c, log_g_c, 0.0)
            log_g_r = jnp.where(valid_r, log_g_r, 0.0)
        gam_c = _dot(tri_incl, log_g_c, precision=HIGHEST)
        gam_r = _dot(log_g_r, tri_upper, precision=HIGHEST)
        for h in heads:
            qh, kh, vh = (qkv_ref[rows, base + h * GDN_DK:base + (h + 1) * GDN_DK].astype(F32)
                          for base in (0, GDN_QK, 2 * GDN_QK))
            qn = qh * lax.rsqrt(jnp.sum(qh * qh, -1, keepdims=True) + RMS_EPS) * (GDN_DK ** -0.5)
            k_n = kh * lax.rsqrt(jnp.sum(kh * kh, -1, keepdims=True) + RMS_EPS)
            b_c = beta[:, h:h + 1]
            g_c = gam_c[:, h:h + 1]
            g_r = gam_r[h:h + 1, :]
            g_last = g_c[c - 1:c, :]
            e_g = jnp.exp(g_c)
            kn_b.append(k_n.astype(BF16))
            qn_b.append(qn.astype(BF16))
            beta_c.append(b_c)
            decay.append(jnp.where(incl, jnp.exp(jnp.where(incl, g_c - g_r, 0.0)), 0.0))
            rhs.append(jnp.concatenate([vh * b_c, k_n * (b_c * e_g)], axis=-1))
            qdec.append(qn * e_g)
            kdt_b.append((k_n * jnp.exp(g_last - g_c)).T.astype(BF16))
            g_tot.append(jnp.exp(g_last))
    pairs = range(n_sub * GDN_HEADS)
    a_mat = [jnp.where(strict, _dot_nt(kn_b[m], kn_b[m]) * decay[m], 0.0) * beta_c[m] for m in pairs]
    qk_b = [(_dot_nt(qn_b[m], kn_b[m]) * decay[m]).astype(BF16) for m in pairs]

    l_mat = [-a for a in a_mat]
    p_b = [a.astype(BF16) for a in a_mat]
    for _ in range(5):
        p = [_dot(b, b) for b in p_b]
        p_b = [x.astype(BF16) for x in p]
        l_mat = [l + x + _dot(l.astype(BF16), xb) for l, x, xb in zip(l_mat, p, p_b)]
    sol = [r + _dot(l.astype(BF16), r.astype(BF16)) for l, r in zip(l_mat, rhs)]
    wq_b = [jnp.concatenate([sol[m][:, GDN_DV:], qdec[m]], axis=0).astype(BF16) for m in pairs]

    for i in range(n_sub):
        of = lambda h: i * GDN_HEADS + h
        state = [s_ref[h] for h in heads]
        p1 = [_dot(wq_b[of(h)], state[h].astype(BF16)) for h in heads]
        u_b = [(sol[of(h)][:, :GDN_DV] - p1[h][:c]).astype(BF16) for h in heads]
        for h in heads:
            s_ref[h] = state[h] * g_tot[of(h)] + _dot(kdt_b[of(h)], u_b[h])
        if o_ref is not None:
            for h in heads:
                o = p1[h][c:] + _dot(qk_b[of(h)], u_b[h])
                o = o * lax.rsqrt(jnp.mean(o * o, -1, keepdims=True) + RMS_EPS) * normw_ref[...]
                gate = z_ref[rows_of[i], h * GDN_DV:(h + 1) * GDN_DV].astype(F32)
                o_ref[rows_of[i], h * GDN_DV:(h + 1) * GDN_DV] = (o * gate).astype(o_ref.dtype)


def _gdn(qkv, z, ba, a_t, s0, a_log, dt_bias, norm_w, bsz, n_chunks, n_pad, emit_state):
    cps = _largest_tile(n_chunks, GDN_CHUNKS_PER_STEP)
    n_steps = n_chunks // cps
    c = cps * CHUNK
    rows = lambda b, n: (b * n_steps + n, 0)
    c2 = lambda b, n: (0, 0)
    c3 = lambda b, n: (0, 0, 0)
    in_specs = [
        pl.BlockSpec((c, qkv.shape[1]), rows),
        pl.BlockSpec((c, z.shape[1]), rows),
        pl.BlockSpec((c, ba.shape[1]), rows),
        pl.BlockSpec((cps, GDN_HEADS, CHUNK), lambda b, n: (b * n_steps + n, 0, 0)),
        pl.BlockSpec(s0.shape, c3),
        pl.BlockSpec((1, GDN_HEADS), c2),
        pl.BlockSpec((1, GDN_HEADS), c2),
        pl.BlockSpec((GDN_HEADS, 1), c2),
        pl.BlockSpec((GDN_HEADS, 1), c2),
        pl.BlockSpec((1, GDN_DV), c2),
    ]
    if emit_state:
        out_specs = pl.BlockSpec(s0.shape, c3)
        out_shape = jax.ShapeDtypeStruct(s0.shape, F32)
    else:
        out_specs = pl.BlockSpec((c, GDN_VD), rows)
        out_shape = jax.ShapeDtypeStruct((qkv.shape[0], GDN_VD), BF16)
    return pl.pallas_call(
        functools.partial(_gdn_kernel, n_pad=n_pad, emit_state=emit_state),
        grid=(bsz, n_steps),
        in_specs=in_specs,
        out_specs=out_specs,
        out_shape=out_shape,
        scratch_shapes=[pltpu.VMEM((GDN_HEADS, GDN_DK, GDN_DV), F32)],
        compiler_params=pltpu.CompilerParams(dimension_semantics=("parallel", "arbitrary"),
                                             vmem_limit_bytes=VMEM_LIMIT),
        name="gdn_meta" if emit_state else "gdn",
    )(qkv, z, ba, a_t, s0, a_log.reshape(1, -1), dt_bias.reshape(1, -1),
      a_log.reshape(-1, 1), dt_bias.reshape(-1, 1), norm_w.reshape(1, -1))


def _swa_kernel(q_ref, kvc_ref, kvp_ref, kvm_ref, sink_ref, o_ref):
    i = pl.program_id(1)
    blk = WINDOW
    hd = SWA_HEAD_DIM
    rows = SWA_GROUP * blk
    t = lax.broadcasted_iota(jnp.int32, (rows, blk), 0) & (blk - 1)
    s = lax.broadcasted_iota(jnp.int32, (rows, blk), 1)
    d_cur = (t - s).astype(F32)
    m_cur = t >= s
    m_prev = jnp.logical_and(s > t, i > 0)
    tm_ = lax.broadcasted_iota(jnp.int32, (rows, N_META), 0) & (blk - 1)
    sm_ = lax.broadcasted_iota(jnp.int32, (rows, N_META), 1)
    d_meta = (tm_ - sm_ + N_META + i * blk).astype(F32)
    ones = jnp.ones((blk, hd), BF16)
    ones_meta = jnp.ones((N_META, hd), BF16)
    groups = range(SWA_KV_HEADS)

    def per_row(vals):
        return jnp.concatenate([jnp.broadcast_to(v, (blk, 1)) for v in vals], axis=0)

    sc, sinks = [], []
    for g in groups:
        hqs = [g * SWA_GROUP + j for j in range(SWA_GROUP)]
        q_g = jnp.concatenate([q_ref[:, hq * hd:(hq + 1) * hd] for hq in hqs], axis=0) * (hd ** -0.5)
        slope = per_row([jnp.full((1, 1), 2.0 ** (-8.0 * (hq + 1) / SWA_Q_HEADS), F32) for hq in hqs])
        bias_cur = slope * d_cur
        sc_cur = jnp.where(m_cur, _dot_nt(q_g, kvc_ref[:, g * hd:(g + 1) * hd]) - bias_cur, NEG_BIG)
        sc_prev = jnp.where(m_prev, _dot_nt(q_g, kvp_ref[:, g * hd:(g + 1) * hd]) - (bias_cur + slope * blk),
                            NEG_BIG)
        sc_meta = _dot_nt(q_g, kvm_ref[:, g * hd:(g + 1) * hd]) - slope * d_meta
        sc.append((sc_cur, sc_prev, sc_meta))
        sinks.append(per_row([sink_ref[:, hq:hq + 1] for hq in hqs]))
    e, e_sink = [], []
    for g in groups:
        sc_cur, sc_prev, sc_meta = sc[g]
        m = jnp.maximum(jnp.maximum(jnp.max(jnp.maximum(sc_cur, sc_prev), -1, keepdims=True),
                                    jnp.max(sc_meta, -1, keepdims=True)), sinks[g])
        e.append(tuple(jnp.exp(x - m).astype(BF16) for x in sc[g]))
        e_sink.append(jnp.exp(sinks[g] - m))
    for g in groups:
        v_of = lambda ref, pad: jnp.concatenate([ref[:, SWA_KVD + g * hd:SWA_KVD + (g + 1) * hd], pad], axis=-1)
        acc = (_dot(e[g][0], v_of(kvc_ref, ones)) + _dot(e[g][1], v_of(kvp_ref, ones))
               + _dot(e[g][2], v_of(kvm_ref, ones_meta)))
        o = acc[:, :hd] / (acc[:, hd:hd + 1] + e_sink[g])
        for j in range(SWA_GROUP):
            hq = g * SWA_GROUP + j
            o_ref[:, hq * hd:(hq + 1) * hd] = o[j * blk:(j + 1) * blk].astype(o_ref.dtype)


def _swa(q, kv, kv_meta, sinks, bsz, nq):
    blk = WINDOW
    return pl.pallas_call(
        _swa_kernel,
        grid=(bsz, nq),
        in_specs=[
            pl.BlockSpec((blk, SWA_QD), lambda b, i: (b * nq + i, 0)),
            pl.BlockSpec((blk, 2 * SWA_KVD), lambda b, i: (b * nq + i, 0)),
            pl.BlockSpec((blk, 2 * SWA_KVD), lambda b, i: (b * nq + jnp.maximum(i - 1, 0), 0)),
            pl.BlockSpec((N_META, 2 * SWA_KVD), lambda b, i: (0, 0)),
            pl.BlockSpec((1, SWA_Q_HEADS), lambda b, i: (0, 0)),
        ],
        out_specs=pl.BlockSpec((blk, SWA_QD), lambda b, i: (b * nq + i, 0)),
        out_shape=jax.ShapeDtypeStruct((q.shape[0], SWA_QD), BF16),
        compiler_params=pltpu.CompilerParams(dimension_semantics=("parallel", "parallel"),
                                             vmem_limit_bytes=VMEM_LIMIT),
        name="swa",
    )(q, kv, kv, kv_meta, sinks.reshape(1, -1))


def _merge_kernel(x_ref, eg_ref, eb_ref, og_ref, os_ref, gates_ref, wg_ref, ws_ref, wo_ref, g1_ref, b1_ref,
                  wr_ref, br_ref, h1_ref, idx_ref, tw_ref):
    d = x_ref.shape[1]
    h0 = _layer_norm(x_ref[...], eg_ref[...], eb_ref[...])
    y_g = _dot(og_ref[...], wg_ref[...])
    y_s = _dot(os_ref[...], ws_ref[...])
    mix = _sigmoid(gates_ref[:, :d].astype(F32)) * y_g + _sigmoid(gates_ref[:, d:].astype(F32)) * y_s
    mixed = _dot(mix.astype(BF16), wo_ref[...])
    h1 = _layer_norm(DEEPNORM_ALPHA * h0 + mixed, g1_ref[...], b1_ref[...])
    _store_token_tiles(h1_ref, h1)

    h_hi, w_hi = h1.astype(BF16), wr_ref[...].astype(BF16)
    h_lo = (h1 - h_hi.astype(F32)).astype(BF16)
    w_lo = (wr_ref[...] - w_hi.astype(F32)).astype(BF16)
    logits = _dot(h_hi, w_hi) + (_dot(h_hi, w_lo) + _dot(h_lo, w_hi)) + br_ref[...]
    lane = lax.broadcasted_iota(jnp.int32, logits.shape, 1)
    vals, idxs = [], []
    for _ in range(TOP_K):
        m = jnp.max(logits, -1, keepdims=True)
        sel = jnp.min(jnp.where(logits == m, lane, LANES), -1, keepdims=True)
        vals.append(m)
        idxs.append(sel)
        logits = jnp.where(lane == sel, NEG_BIG, logits)
    e = [jnp.exp(v - vals[0]) for v in vals]
    tot = e[0] + e[1] + e[2] + e[3]
    for k in range(TOP_K):
        idx_ref[:, k:k + 1] = idxs[k]
        tw_ref[:, k:k + 1] = e[k] / tot


def _merge(x2d, eg, eb, o_gdn, o_swa, gates, wg, ws, wo, g1, b1, wr, br, tm):
    n, d = x2d.shape
    row = lambda i: (i, 0)
    const = lambda i: (0, 0)
    full = lambda a: pl.BlockSpec(a.shape, const)
    return pl.pallas_call(
        _merge_kernel,
        grid=(n // tm,),
        in_specs=[
            pl.BlockSpec((tm, d), row), full(eg), full(eb),
            pl.BlockSpec((tm, GDN_VD), row), pl.BlockSpec((tm, SWA_QD), row), pl.BlockSpec((tm, 2 * d), row),
            full(wg), full(ws), full(wo), full(g1), full(b1), full(wr), full(br),
        ],
        out_specs=[
            pl.BlockSpec((tm * _SUB, LANES), row), pl.BlockSpec((tm, TOP_K), row), pl.BlockSpec((tm, TOP_K), row),
        ],
        out_shape=[
            jax.ShapeDtypeStruct((n * _SUB, LANES), F32),
            jax.ShapeDtypeStruct((n, TOP_K), jnp.int32), jax.ShapeDtypeStruct((n, TOP_K), F32),
        ],
        compiler_params=pltpu.CompilerParams(dimension_semantics=("parallel",), vmem_limit_bytes=VMEM_LIMIT),
        name="merge",
    )(x2d, eg, eb, o_gdn, o_swa, gates, wg, ws, wo, g1, b1, wr, br)


def _rank_kernel(idx_ref, rank_ref, counts_ref, run_ref):
    t = idx_ref.shape[0]

    @pl.when(pl.program_id(0) == 0)
    def _():
        run_ref[...] = jnp.zeros_like(run_ref)

    lane = lax.broadcasted_iota(jnp.int32, (t, LANES), 1)
    hits = [idx_ref[:, k:k + 1] == lane for k in range(TOP_K)]
    onehot = hits[0].astype(F32)
    for k in range(1, TOP_K):
        onehot = onehot + hits[k].astype(F32)
    r = lax.broadcasted_iota(jnp.int32, (t, t), 0)
    c = lax.broadcasted_iota(jnp.int32, (t, t), 1)
    before = (r > c).astype(BF16)
    base = run_ref[...] + _dot(before, onehot.astype(BF16))
    for k in range(TOP_K):
        rank_ref[:, k:k + 1] = jnp.sum(jnp.where(hits[k], base, 0.0), -1, keepdims=True).astype(jnp.int32)
    run_ref[...] = run_ref[...] + jnp.sum(onehot, 0, keepdims=True)
    counts_ref[...] = run_ref[...].astype(jnp.int32)


def _rank(top_idx, t):
    n = top_idx.shape[0]
    return pl.pallas_call(
        _rank_kernel,
        grid=(n // t,),
        in_specs=[pl.BlockSpec((t, TOP_K), lambda i: (i, 0))],
        out_specs=[pl.BlockSpec((t, TOP_K), lambda i: (i, 0)), pl.BlockSpec((1, LANES), lambda i: (0, 0))],
        out_shape=[jax.ShapeDtypeStruct((n, TOP_K), jnp.int32), jax.ShapeDtypeStruct((1, LANES), jnp.int32)],
        scratch_shapes=[pltpu.VMEM((1, LANES), F32)],
        compiler_params=pltpu.CompilerParams(dimension_semantics=("arbitrary",)),
        name="rank",
    )(top_idx)


_ROW_GROUP = 8


def _row_copy(src, dst, sem):
    return pltpu.make_async_copy(src, dst, sem)


def _dispatch_kernel(pad_end_ref, dest_hbm, h_ref, xs_hbm, idx_smem, zeros_ref, sem_idx, sem_rows, sem_zero):
    i = pl.program_id(0)
    t = h_ref.shape[0] // _SUB
    n_idx = t * TOP_K

    @pl.when(i == 0)
    def _():
        zeros_ref[...] = jnp.zeros_like(zeros_ref)
        for e in range(N_EXPERTS):
            end = pad_end_ref[e]
            start = pad_end_ref[e - 1] if e else 0

            @pl.when(end > start)
            def _():
                tile0 = pl.multiple_of(end - MOE_TILE, MOE_TILE)
                fill = _row_copy(zeros_ref, xs_hbm.at[pl.ds(tile0, MOE_TILE)], sem_zero)
                fill.start()
                fill.wait()

    idx_cp = _row_copy(dest_hbm.at[pl.ds(pl.multiple_of(i * n_idx, n_idx), n_idx)], idx_smem, sem_idx)
    idx_cp.start()
    idx_cp.wait()

    def issue(grp, carry):
        row0 = pl.multiple_of(grp * _ROW_GROUP, _ROW_GROUP)
        for r in range(_ROW_GROUP):
            for k in range(TOP_K):
                dst_row = idx_smem[(row0 + r) * TOP_K + k]
                _row_copy(h_ref.at[pl.ds((row0 + r) * _SUB, _SUB)], xs_hbm.at[dst_row],
                          sem_rows).start(priority=k % 2)
        return carry

    lax.fori_loop(0, t // _ROW_GROUP, issue, 0)
    _row_copy(xs_hbm.at[pl.ds(0, n_idx)], xs_hbm.at[pl.ds(0, n_idx)], sem_rows).wait()


def _dispatch(pad_end, dest_flat, h1t, n_rows, t):
    n = h1t.shape[0] // _SUB
    grid_spec = pltpu.PrefetchScalarGridSpec(
        num_scalar_prefetch=1,
        grid=(n // t,),
        in_specs=[
            pl.BlockSpec(memory_space=pl.ANY),
            pl.BlockSpec((t * _SUB, LANES), lambda i, pe: (i, 0)),
        ],
        out_specs=pl.BlockSpec(memory_space=pl.ANY),
        scratch_shapes=[
            pltpu.SMEM((t * TOP_K,), jnp.int32),
            pltpu.VMEM((MOE_TILE, _SUB, LANES), h1t.dtype),
            pltpu.SemaphoreType.DMA, pltpu.SemaphoreType.DMA, pltpu.SemaphoreType.DMA,
        ],
    )
    return pl.pallas_call(
        _dispatch_kernel,
        grid_spec=grid_spec,
        out_shape=jax.ShapeDtypeStruct((n_rows, _SUB, LANES), h1t.dtype),
        compiler_params=pltpu.CompilerParams(dimension_semantics=("arbitrary",), vmem_limit_bytes=VMEM_LIMIT),
        name="dispatch",
    )(pad_end, dest_flat, h1t)


def _expert_kernel(blk_e_ref, n_used_ref, xs_ref, w1_ref, b1_ref, w2_ref, b2_ref, ys_ref):
    del blk_e_ref
    i = pl.program_id(0)
    d_ff = w2_ref.shape[0]

    @pl.when(i < n_used_ref[0])
    def _():
        x = _load_token_tiles(xs_ref, MOE_TILE)
        hdn = _dot(x.astype(BF16), w1_ref[...].astype(BF16)) + b1_ref[...]
        gate = jnp.minimum(hdn[:, :d_ff], SWIGLU_LIMIT)
        up = jnp.clip(hdn[:, d_ff:], -SWIGLU_LIMIT, SWIGLU_LIMIT)
        act = gate * _sigmoid(SWIGLU_ALPHA * gate) * (up + 1.0)
        _store_token_tiles(ys_ref, _dot(act.astype(BF16), w2_ref[...].astype(BF16)) + b2_ref[...])

    @pl.when(i >= n_used_ref[0])
    def _():
        ys_ref[...] = jnp.zeros_like(ys_ref)


def _experts(blk_e, n_used, xs, w1, b1, w2, b2):
    n_e, d, d_ff2 = w1.shape
    d_ff = w2.shape[1]
    n_rows, w = xs.shape[0] // _SUB, LANES
    tile = MOE_TILE * _SUB
    grid_spec = pltpu.PrefetchScalarGridSpec(
        num_scalar_prefetch=2,
        grid=(n_rows // MOE_TILE,),
        in_specs=[
            pl.BlockSpec((tile, w), lambda i, be, nu: (jnp.minimum(i, nu[0] - 1), 0)),
            pl.BlockSpec((None, d, d_ff2), lambda i, be, nu: (be[i], 0, 0)),
            pl.BlockSpec((None, 1, d_ff2), lambda i, be, nu: (be[i], 0, 0)),
            pl.BlockSpec((None, d_ff, d), lambda i, be, nu: (be[i], 0, 0)),
            pl.BlockSpec((None, 1, d), lambda i, be, nu: (be[i], 0, 0)),
        ],
        out_specs=pl.BlockSpec((tile, w), lambda i, be, nu: (i, 0)),
    )
    return pl.pallas_call(
        _expert_kernel,
        grid_spec=grid_spec,
        out_shape=jax.ShapeDtypeStruct((n_rows * _SUB, w), F32),
        compiler_params=pltpu.CompilerParams(dimension_semantics=("arbitrary",), vmem_limit_bytes=VMEM_LIMIT),
        name="experts",
    )(blk_e, n_used, xs, w1, b1.reshape(n_e, 1, d_ff2), w2, b2.reshape(n_e, 1, d))


def _combine_kernel(dest_hbm, ys_hbm, h1_ref, tw_ref, g_ref, b_ref, out_ref, idx_smem, buf, sem_idx, sem_rows):
    i = pl.program_id(0)
    t = out_ref.shape[0]
    n_idx = t * TOP_K

    def start_gather(step, slot):
        idx_cp = _row_copy(dest_hbm.at[pl.ds(pl.multiple_of(step * n_idx, n_idx), n_idx)], idx_smem, sem_idx)
        idx_cp.start()
        idx_cp.wait()

        def issue(grp, carry):
            row0 = pl.multiple_of(grp * _ROW_GROUP, _ROW_GROUP)
            for r in range(_ROW_GROUP):
                for k in range(TOP_K):
                    src_row = idx_smem[(row0 + r) * TOP_K + k]
                    _row_copy(ys_hbm.at[src_row], buf.at[slot, k, pl.ds((row0 + r) * _SUB, _SUB)],
                              sem_rows.at[slot]).start(priority=k % 2)
            return carry

        lax.fori_loop(0, t // _ROW_GROUP, issue, 0)

    @pl.when(i == 0)
    def _():
        start_gather(0, 0)

    @pl.when(i + 1 < pl.num_programs(0))
    def _():
        start_gather(i + 1, (i + 1) % 2)

    slot = i % 2
    _row_copy(buf.at[slot], buf.at[slot], sem_rows.at[slot]).wait()

    pre = DEEPNORM_ALPHA * _load_token_tiles(h1_ref, t)
    for k in range(TOP_K):
        pre = pre + _load_token_tiles(buf, t, lead=(slot, k)) * tw_ref[:, k:k + 1]
    out_ref[...] = _layer_norm(pre, g_ref[...], b_ref[...])


def _combine(dest_flat, ys, h1t, top_w, g2, b2, t):
    n, d = h1t.shape[0] // _SUB, _SUB * LANES
    return pl.pallas_call(
        _combine_kernel,
        grid=(n // t,),
        in_specs=[
            pl.BlockSpec(memory_space=pl.ANY),
            pl.BlockSpec(memory_space=pl.ANY),
            pl.BlockSpec((t * _SUB, LANES), lambda i: (i, 0)),
            pl.BlockSpec((t, TOP_K), lambda i: (i, 0)),
            pl.BlockSpec((1, d), lambda i: (0, 0)),
            pl.BlockSpec((1, d), lambda i: (0, 0)),
        ],
        out_specs=pl.BlockSpec((t, d), lambda i: (i, 0)),
        out_shape=jax.ShapeDtypeStruct((n, d), F32),
        scratch_shapes=[
            pltpu.SMEM((t * TOP_K,), jnp.int32),
            pltpu.VMEM((2, TOP_K, t * _SUB, LANES), ys.dtype),
            pltpu.SemaphoreType.DMA,
            pltpu.SemaphoreType.DMA((2,)),
        ],
        compiler_params=pltpu.CompilerParams(dimension_semantics=("arbitrary",), vmem_limit_bytes=VMEM_LIMIT),
        name="combine",
    )(dest_flat, ys, h1t, top_w, g2, b2)


def _largest_tile(n, cap):
    t = cap
    while n % t:
        t //= 2
    return t


def kernel(x, meta_tokens, emb_ln_g, emb_ln_b, w_in, conv_w, a_log, dt_bias, gdn_norm_w, attn_sinks, w_br_gdn,
           w_br_swa, w_out, ln1_g, ln1_b, w_router, b_router, w_moe1, b_moe1, w_moe2, b_moe2, ln2_g, ln2_b):
    bsz, seq, d = x.shape
    assert seq % WINDOW == 0 and seq % CHUNK == 0 and d == GDN_VD == _SUB * LANES
    n = bsz * seq
    x2d = x.reshape(n, d)
    row1 = lambda v: v.reshape(1, -1)
    l = 0

    sp = np.cumsum((GDN_QK, GDN_QK, GDN_VD, GDN_VD, GDN_HEADS, GDN_HEADS, SWA_QD, SWA_KVD, SWA_KVD, d, d))
    w = w_in[l]
    w_perm = jnp.concatenate([
        w[:, :sp[3]], w[:, sp[5]:sp[10]], w[:, sp[3]:sp[5]],
        jnp.zeros((d, LANES - 2 * GDN_HEADS), w.dtype)], axis=1).astype(BF16)
    eg, eb = row1(emb_ln_g), row1(emb_ln_b)

    tm = _largest_tile(seq, 512)
    conv_w_t = conv_w[l].T
    m_qkv, _, _, m_kv, _, m_ba, m_tail = _in_proj(
        meta_tokens.astype(x.dtype), eg, eb, w_perm, jnp.zeros((_HIST, _C_QKVG), F32), conv_w_t, N_META, 1, True)
    qkv_g, z_g, q_s, kv_s, gates, ba = _in_proj(x2d, eg, eb, w_perm, m_tail, conv_w_t, tm, seq // tm, False)

    n_pad = CHUNK - N_META
    n_chunks = seq // CHUNK
    pad_rows = lambda a: jnp.pad(a, ((n_pad, 0), (0, 0)))
    chunk_t = lambda a, nb: a[:, GDN_HEADS:2 * GDN_HEADS].reshape(nb, CHUNK, GDN_HEADS).transpose(0, 2, 1)
    m_ba_p = pad_rows(m_ba)
    zeros_state = jnp.zeros((GDN_HEADS, GDN_DK, GDN_DV), F32)
    s_meta = _gdn(pad_rows(m_qkv), jnp.zeros((CHUNK, GDN_VD), BF16), m_ba_p, chunk_t(m_ba_p, 1), zeros_state,
                  a_log[l], dt_bias[l], gdn_norm_w[l], 1, 1, n_pad, True)
    o_gdn = _gdn(qkv_g, z_g, ba, chunk_t(ba, bsz * n_chunks), s_meta, a_log[l], dt_bias[l], gdn_norm_w[l],
                 bsz, n_chunks, 0, False)

    o_swa = _swa(q_s, kv_s, m_kv, attn_sinks[l], bsz, seq // WINDOW)

    wr = jnp.pad(w_router[l], ((0, 0), (0, LANES - N_EXPERTS)))
    br = jnp.pad(row1(b_router[l]), ((0, 0), (0, LANES - N_EXPERTS)), constant_values=NEG_BIG)
    h1, top_idx, top_w = _merge(
        x2d, eg, eb, o_gdn, o_swa, gates, w_br_gdn[l].astype(BF16), w_br_swa[l].astype(BF16),
        w_out[l].astype(BF16), row1(ln1_g[l]), row1(ln1_b[l]), wr, br, tm)

    rank, counts = _rank(top_idx, _largest_tile(n, 512))
    counts = counts[0, :N_EXPERTS]
    padded = (counts + MOE_TILE - 1) // MOE_TILE * MOE_TILE
    pad_end = jnp.cumsum(padded)
    pad_start = pad_end - padded
    dest = (pad_start[top_idx] + rank).reshape(-1)
    n_blk = -(-n * TOP_K // MOE_TILE) + N_EXPERTS
    blk_row0 = jnp.arange(n_blk, dtype=jnp.int32) * MOE_TILE
    blk_e = jnp.minimum(jnp.sum((pad_end[None, :] <= blk_row0[:, None]).astype(jnp.int32), axis=1), N_EXPERTS - 1)
    n_used = (pad_end[-1:] // MOE_TILE).astype(jnp.int32)

    tg = _largest_tile(n, GATHER_TOKENS)
    n_rows = n_blk * MOE_TILE
    xs = _dispatch(pad_end.astype(jnp.int32), dest, h1, n_rows, tg)
    ys = _experts(blk_e, n_used, xs.reshape(n_rows * _SUB, LANES), w_moe1[l], b_moe1[l], w_moe2[l], b_moe2[l])
    out = _combine(dest, ys.reshape(n_rows, _SUB, LANES), h1, top_w, row1(ln2_g[l]), row1(ln2_b[l]), tg)
    return out.reshape(bsz, seq, d)
```

```python
import functools

import jax
import jax.numpy as jnp
import numpy as np
from jax import lax
from jax.experimental import pallas as pl
from jax.experimental.pallas import tpu as pltpu

N_META = 16
GDN_HEADS = 8
GDN_DK = 128
GDN_DV = 128
CONV_WIDTH = 4
CHUNK = 64
SWA_Q_HEADS = 16
SWA_KV_HEADS = 4
SWA_HEAD_DIM = 64
SWA_GROUP = SWA_Q_HEADS // SWA_KV_HEADS
WINDOW = 128
N_EXPERTS = 32
TOP_K = 4
SWIGLU_ALPHA = 1.702
SWIGLU_LIMIT = 7.0
LN_EPS = 1e-5
RMS_EPS = 1e-6
DEPTH = 1
DEEPNORM_ALPHA = (2.0 * DEPTH) ** 0.25

GDN_QK = GDN_HEADS * GDN_DK
GDN_VD = GDN_HEADS * GDN_DV
SWA_QD = SWA_Q_HEADS * SWA_HEAD_DIM
SWA_KVD = SWA_KV_HEADS * SWA_HEAD_DIM

LANES = 128
MOE_TILE = 512
GATHER_TOKENS = 512
NEG_BIG = -1e30
VMEM_LIMIT = 56 * 1024 * 1024

F32 = jnp.float32
BF16 = jnp.bfloat16
HIGHEST = lax.Precision.HIGHEST


def _layer_norm(x, g, b):
    mu = jnp.mean(x, -1, keepdims=True)
    xc = x - mu
    var = jnp.mean(xc * xc, -1, keepdims=True)
    return xc * lax.rsqrt(var + LN_EPS) * g + b


def _sigmoid(x):
    return 1.0 / (1.0 + jnp.exp(-x))


def _silu(x):
    return x * _sigmoid(x)


def _softplus(x):
    return jnp.maximum(x, 0.0) + jnp.log(1.0 + jnp.exp(-jnp.abs(x)))


def _dot(a, b, **kw):
    return jnp.dot(a, b, preferred_element_type=F32, **kw)


_SUB = 8


def _load_token_tiles(ref, t, lead=()):
    return jnp.concatenate([ref[(*lead, pl.ds(j, t, stride=_SUB), slice(None))] for j in range(_SUB)], axis=-1)


def _store_token_tiles(ref, x):
    for j in range(_SUB):
        ref[pl.ds(j, x.shape[0], stride=_SUB), :] = x[:, j * LANES:(j + 1) * LANES]


def _dot_nt(a, b):
    return lax.dot_general(a, b, (((1,), (1,)), ((), ())), preferred_element_type=F32)


_C_QKVG = 2 * GDN_QK + GDN_VD
_C_Z = _C_QKVG + GDN_VD
_C_QS = _C_Z + SWA_QD
_C_KVS = _C_QS + 2 * SWA_KVD
_C_GATES = _C_KVS + 2 * GDN_VD
_C_BA = _C_GATES + LANES


_HIST = 8
_CONV_COLS = 512


def _inproj_kernel(x_ref, g_ref, b_ref, w_ref, hist0_ref, convw_ref, qkvg_ref, z_ref, qs_ref, kvs_ref, gates_ref,
                   ba_ref, *rest, tiles_per_seq, emit_tail):
    win_ref = rest[-1]
    tm = x_ref.shape[0]
    h = _layer_norm(x_ref[...], g_ref[...], b_ref[...]).astype(BF16)

    @pl.when(pl.program_id(0) % tiles_per_seq == 0)
    def _():
        win_ref[0:_HIST, :] = hist0_ref[...]

    def proj(out_ref, lo, c, ce, act=None):
        y = _dot(h, w_ref[:, c:ce])
        out_ref[:, c - lo:ce - lo] = (y if act is None else act(y)).astype(out_ref.dtype)

    def conv(c):
        cols = slice(c, c + _CONV_COLS)
        acc = win_ref[_HIST:_HIST + tm, cols] * convw_ref[CONV_WIDTH - 1:CONV_WIDTH, cols]
        for i in range(CONV_WIDTH - 1):
            sh = CONV_WIDTH - 1 - i
            acc = acc + win_ref[_HIST - sh:_HIST - sh + tm, cols] * convw_ref[i:i + 1, cols]
        qkvg_ref[:, cols] = _silu(acc).astype(qkvg_ref.dtype)

    for c in range(0, _C_QKVG, 1024):
        win_ref[_HIST:_HIST + tm, c:c + 1024] = _dot(h, w_ref[:, c:c + 1024])
    if emit_tail:
        rest[0][...] = win_ref[tm:tm + _HIST, :]
    others = [(z_ref, _C_QKVG, _C_Z, _silu), (qs_ref, _C_Z, _C_QS, None), (kvs_ref, _C_QS, _C_KVS, None),
              (gates_ref, _C_KVS, _C_GATES, None), (ba_ref, _C_GATES, _C_BA, None)]
    mm = [(ref, lo, c, min(c + 1024, hi), act) for ref, lo, hi, act in others for c in range(lo, hi, 1024)]
    cv = list(range(0, _C_QKVG, _CONV_COLS))
    for k in range(max(len(mm), len(cv))):
        if k < len(mm):
            proj(*mm[k])
        if k < len(cv):
            conv(cv[k])
    win_ref[0:_HIST, :] = win_ref[tm:tm + _HIST, :]


def _in_proj(x2d, ln_g, ln_b, w_perm, hist0, conv_w_t, tm, tiles_per_seq, emit_tail):
    n, d = x2d.shape
    widths = (_C_QKVG, GDN_VD, SWA_QD, 2 * SWA_KVD, _C_GATES - _C_KVS, LANES)
    dtypes = (BF16, BF16, BF16, BF16, BF16, F32)
    row = lambda i: (i, 0)
    const = lambda i: (0, 0)
    out_specs = [pl.BlockSpec((tm, w), row) for w in widths]
    out_shape = [jax.ShapeDtypeStruct((n, w), dt) for w, dt in zip(widths, dtypes)]
    if emit_tail:
        out_specs.append(pl.BlockSpec((_HIST, _C_QKVG), const))
        out_shape.append(jax.ShapeDtypeStruct((_HIST, _C_QKVG), F32))
    return pl.pallas_call(
        functools.partial(_inproj_kernel, tiles_per_seq=tiles_per_seq, emit_tail=emit_tail),
        grid=(n // tm,),
        in_specs=[
            pl.BlockSpec((tm, d), row),
            pl.BlockSpec((1, d), const),
            pl.BlockSpec((1, d), const),
            pl.BlockSpec((d, _C_BA), const, pipeline_mode=pl.Buffered(1)),
            pl.BlockSpec((_HIST, _C_QKVG), const),
            pl.BlockSpec((CONV_WIDTH, _C_QKVG), const),
        ],
        out_specs=out_specs,
        out_shape=out_shape,
        scratch_shapes=[pltpu.VMEM((_HIST + tm, _C_QKVG), F32)],
        compiler_params=pltpu.CompilerParams(dimension_semantics=("arbitrary",), vmem_limit_bytes=VMEM_LIMIT),
        name="in_proj",
    )(x2d, ln_g, ln_b, w_perm, hist0, conv_w_t)


GDN_CHUNKS_PER_STEP = 4
GDN_CHUNKS_PER_BLOCK = 4


def _gdn_kernel(qkv_ref, z_ref, ba_ref, at_ref, s0_ref, alog_r_ref, dtb_r_ref, alog_c_ref, dtb_c_ref, normw_ref,
                out_ref, s_ref, *, n_pad, emit_state):
    chunks_per_step = qkv_ref.shape[0] // CHUNK
    n_sub = min(GDN_CHUNKS_PER_BLOCK, chunks_per_step)

    @pl.when(pl.program_id(1) == 0)
    def _():
        s_ref[...] = s0_ref[...]

    def block(j, carry):
        _gdn_chunks(j, n_sub, qkv_ref, z_ref, ba_ref, at_ref, alog_r_ref, dtb_r_ref, alog_c_ref, dtb_c_ref,
                    normw_ref, None if emit_state else out_ref, s_ref, n_pad)
        return carry

    if chunks_per_step == n_sub:
        block(0, 0)
    else:
        lax.fori_loop(0, chunks_per_step // n_sub, block, 0)
    if emit_state:
        out_ref[...] = s_ref[...]


def _gdn_chunks(j, n_sub, qkv_ref, z_ref, ba_ref, at_ref, alog_r_ref, dtb_r_ref, alog_c_ref, dtb_c_ref,
                normw_ref, o_ref, s_ref, n_pad):
    c = CHUNK
    heads = range(GDN_HEADS)
    row = lax.broadcasted_iota(jnp.int32, (c, c), 0)
    col = lax.broadcasted_iota(jnp.int32, (c, c), 1)
    incl = row >= col
    strict = row > col
    tri_incl = incl.astype(F32)
    tri_upper = (row <= col).astype(F32)

    rows_of, kn_b, qn_b, decay, beta_c, rhs, qdec, kdt_b, g_tot = [], [], [], [], [], [], [], [], []
    for i in range(n_sub):
        ci = j * n_sub + i
        rows = pl.ds(ci * c, c) if isinstance(ci, int) else pl.ds(pl.multiple_of(ci * c, c), c)
        rows_of.append(rows)
        beta = _sigmoid(ba_ref[rows, 0:GDN_HEADS])
        log_g_c = -jnp.exp(alog_r_ref[...]) * _softplus(ba_ref[rows, GDN_HEADS:2 * GDN_HEADS] + dtb_r_ref[...])
        log_g_r = -jnp.exp(alog_c_ref[...]) * _softplus(at_ref[ci] + dtb_c_ref[...])
        if n_pad:
            valid_c = lax.broadcasted_iota(jnp.int32, (c, GDN_HEADS), 0) >= n_pad
            valid_r = lax.broadcasted_iota(jnp.int32, (GDN_HEADS, c), 1) >= n_pad
            beta = jnp.where(valid_c, beta, 0.0)
            log_g_c = jnp.where(valid_c, log_g_c, 0.0)
            log_g_r = jnp.where(valid_r, log_g_r, 0.0)
        gam_c = _dot(tri_incl, log_g_c, precision=HIGHEST)
        gam_r = _dot(log_g_r, tri_upper, precision=HIGHEST)
        for h in heads:
            qh, kh, vh = (qkv_ref[rows, base + h * GDN_DK:base + (h + 1) * GDN_DK].astype(F32)
                          for base in (0, GDN_QK, 2 * GDN_QK))
            qn = qh * lax.rsqrt(jnp.sum(qh * qh, -1, keepdims=True) + RMS_EPS) * (GDN_DK ** -0.5)
            k_n = kh * lax.rsqrt(jnp.sum(kh * kh, -1, keepdims=True) + RMS_EPS)
            b_c = beta[:, h:h + 1]
            g_c = gam_c[:, h:h + 1]
            g_r = gam_r[h:h + 1, :]
            g_last = g_c[c - 1:c, :]
            e_g = jnp.exp(g_c)
            kn_b.append(k_n.astype(BF16))
            qn_b.append(qn.astype(BF16))
            beta_c.append(b_c)
            decay.append(jnp.where(incl, jnp.exp(jnp.where(incl, g_c - g_r, 0.0)), 0.0))
            rhs.append(jnp.concatenate([vh * b_c, k_n * (b_c * e_g)], axis=-1))
            qdec.append(qn * e_g)
            kdt_b.append((k_n * jnp.exp(g_last - g_c)).T.astype(BF16))
            g_tot.append(jnp.exp(g_last))
    pairs = range(n_sub * GDN_HEADS)
    a_mat = [jnp.where(strict, _dot_nt(kn_b[m], kn_b[m]) * decay[m], 0.0) * beta_c[m] for m in pairs]
    qk_b = [(_dot_nt(qn_b[m], kn_b[m]) * decay[m]).astype(BF16) for m in pairs]

    l_mat = [-a for a in a_mat]
    p_b = [a.astype(BF16) for a in a_mat]
    for _ in range(5):
        p = [_dot(b, b) for b in p_b]
        p_b = [x.astype(BF16) for x in p]
        l_mat = [l + x + _dot(l.astype(BF16), xb) for l, x, xb in zip(l_mat, p, p_b)]
    sol = [r + _dot(l.astype(BF16), r.astype(BF16)) for l, r in zip(l_mat, rhs)]
    wq_b = [jnp.concatenate([sol[m][:, GDN_DV:], qdec[m]], axis=0).astype(BF16) for m in pairs]

    for i in range(n_sub):
        of = lambda h: i * GDN_HEADS + h
        state = [s_ref[h] for h in heads]
        p1 = [_dot(wq_b[of(h)], state[h].astype(BF16)) for h in heads]
        u_b = [(sol[of(h)][:, :GDN_DV] - p1[h][:c]).astype(BF16) for h in heads]
        for h in heads:
            s_ref[h] = state[h] * g_tot[of(h)] + _dot(kdt_b[of(h)], u_b[h])
        if o_ref is not None:
            for h in heads:
                o = p1[h][c:] + _dot(qk_b[of(h)], u_b[h])
                o = o * lax.rsqrt(jnp.mean(o * o, -1, keepdims=True) + RMS_EPS) * normw_ref[...]
                gate = z_ref[rows_of[i], h * GDN_DV:(h + 1) * GDN_DV].astype(F32)
                o_ref[rows_of[i], h * GDN_DV:(h + 1) * GDN_DV] = (o * gate).astype(o_ref.dtype)


def _gdn(qkv, z, ba, a_t, s0, a_log, dt_bias, norm_w, bsz, n_chunks, n_pad, emit_state):
    cps = _largest_tile(n_chunks, GDN_CHUNKS_PER_STEP)
    n_steps = n_chunks // cps
    c = cps * CHUNK
    rows = lambda b, n: (b * n_steps + n, 0)
    c2 = lambda b, n: (0, 0)
    c3 = lambda b, n: (0, 0, 0)
    in_specs = [
        pl.BlockSpec((c, qkv.shape[1]), rows),
        pl.BlockSpec((c, z.shape[1]), rows),
        pl.BlockSpec((c, ba.shape[1]), rows),
        pl.BlockSpec((cps, GDN_HEADS, CHUNK), lambda b, n: (b * n_steps + n, 0, 0)),
        pl.BlockSpec(s0.shape, c3),
        pl.BlockSpec((1, GDN_HEADS), c2),
        pl.BlockSpec((1, GDN_HEADS), c2),
        pl.BlockSpec((GDN_HEADS, 1), c2),
        pl.BlockSpec((GDN_HEADS, 1), c2),
        pl.BlockSpec((1, GDN_DV), c2),
    ]
    if emit_state:
        out_specs = pl.BlockSpec(s0.shape, c3)
        out_shape = jax.ShapeDtypeStruct(s0.shape, F32)
    else:
        out_specs = pl.BlockSpec((c, GDN_VD), rows)
        out_shape = jax.ShapeDtypeStruct((qkv.shape[0], GDN_VD), BF16)
    return pl.pallas_call(
        functools.partial(_gdn_kernel, n_pad=n_pad, emit_state=emit_state),
        grid=(bsz, n_steps),
        in_specs=in_specs,
        out_specs=out_specs,
        out_shape=out_shape,
        scratch_shapes=[pltpu.VMEM((GDN_HEADS, GDN_DK, GDN_DV), F32)],
        compiler_params=pltpu.CompilerParams(dimension_semantics=("parallel", "arbitrary"),
                                             vmem_limit_bytes=VMEM_LIMIT),
        name="gdn_meta" if emit_state else "gdn",
    )(qkv, z, ba, a_t, s0, a_log.reshape(1, -1), dt_bias.reshape(1, -1),
      a_log.reshape(-1, 1), dt_bias.reshape(-1, 1), norm_w.reshape(1, -1))


SWA_BLOCKS_PER_STEP = 2


_META_SLOTS = 32


def _swa_tables(sinks):
    blk = WINDOW
    rows = SWA_GROUP * blk
    t = (np.arange(rows) % blk)[:, None]
    s = np.arange(blk)[None, :]
    slope = 2.0 ** (-8.0 * (np.arange(SWA_Q_HEADS) + 1) / SWA_Q_HEADS)
    sl = np.repeat(slope.reshape(SWA_KV_HEADS, SWA_GROUP), blk, axis=1)[:, :, None]
    cur = np.where(t >= s, -sl * (t - s), NEG_BIG)
    prev = np.where(s > t, -sl * (t - s + blk), NEG_BIG)
    band = np.stack([cur, prev], 1).astype(np.float32)
    unused = _META_SLOTS - N_META - 1
    meta0 = np.concatenate([-sl * (t - np.arange(N_META)[None, :] + N_META),
                            np.zeros((SWA_KV_HEADS, rows, 1)),
                            np.full((SWA_KV_HEADS, rows, unused), NEG_BIG)], axis=-1).astype(np.float32)
    meta1 = np.concatenate([np.broadcast_to(sl, (SWA_KV_HEADS, rows, N_META)),
                            np.zeros((SWA_KV_HEADS, rows, 1 + unused))], axis=-1).astype(np.float32)
    sink_rows = jnp.repeat(sinks.astype(F32).reshape(SWA_KV_HEADS, SWA_GROUP), blk, axis=1)
    is_sink = (np.arange(_META_SLOTS) == N_META)[None, None, :]
    meta0 = jnp.where(is_sink, sink_rows[:, :, None], jnp.asarray(meta0))
    return jnp.asarray(band), jnp.stack([meta0, jnp.asarray(meta1)], 1)


def _swa_kernel(q_ref, kvc_ref, kvp_ref, kvm_ref, band_ref, meta_ref, o_ref):
    i = pl.program_id(1)
    blk = WINDOW
    hd = SWA_HEAD_DIM
    n_sub = q_ref.shape[0] // blk
    ones = jnp.ones((blk, hd), BF16)
    ones_meta = jnp.ones((_META_SLOTS, hd), BF16)
    no_prev = jnp.where(i > 0, 0.0, NEG_BIG)
    masked = jnp.full((SWA_GROUP * blk, blk - _META_SLOTS), NEG_BIG, F32)
    pairs = [(u, g) for u in range(n_sub) for g in range(SWA_KV_HEADS)]

    def key_blocks(u):
        cur = (kvc_ref, slice(u * blk, (u + 1) * blk))
        return cur, ((kvp_ref, slice(0, blk)) if u == 0 else (kvc_ref, slice((u - 1) * blk, u * blk)))

    sc = []
    for u, g in pairs:
        (c_ref, c_rows), (p_ref, p_rows) = key_blocks(u)
        kcol = slice(g * hd, (g + 1) * hd)
        q_g = jnp.concatenate([q_ref[u * blk:(u + 1) * blk, (g * SWA_GROUP + j) * hd:(g * SWA_GROUP + j + 1) * hd]
                               for j in range(SWA_GROUP)], axis=0) * (hd ** -0.5)
        sc_cur = _dot_nt(q_g, c_ref[c_rows, kcol]) + band_ref[g, 0]
        sc_prev = _dot_nt(q_g, p_ref[p_rows, kcol]) + band_ref[g, 1]
        if u == 0:
            sc_prev = sc_prev + no_prev
        start = ((i * n_sub + u) * blk).astype(F32)
        sc_meta = _dot_nt(q_g, kvm_ref[:, kcol]) + (meta_ref[g, 0] - meta_ref[g, 1] * start)
        sc.append((sc_cur, sc_prev, sc_meta))
    e = []
    for sc_cur, sc_prev, sc_meta in sc:
        widened = jnp.concatenate([sc_meta, masked], axis=-1)
        m = jnp.max(jnp.maximum(jnp.maximum(sc_cur, sc_prev), widened), -1, keepdims=True)
        e.append(tuple(jnp.exp(x - m).astype(BF16) for x in (sc_cur, sc_prev, sc_meta)))
    for m_, (u, g) in enumerate(pairs):
        (c_ref, c_rows), (p_ref, p_rows) = key_blocks(u)
        vcol = slice(SWA_KVD + g * hd, SWA_KVD + (g + 1) * hd)
        e_cur, e_prev, e_meta = e[m_]
        num = _dot(e_cur, c_ref[c_rows, vcol]) + _dot(e_prev, p_ref[p_rows, vcol]) + _dot(e_meta, kvm_ref[:, vcol])
        den = _dot(e_cur, ones) + _dot(e_prev, ones) + _dot(e_meta, ones_meta)
        o = num / den
        for j in range(SWA_GROUP):
            hq = g * SWA_GROUP + j
            o_ref[u * blk:(u + 1) * blk, hq * hd:(hq + 1) * hd] = o[j * blk:(j + 1) * blk].astype(o_ref.dtype)


def _swa(q, kv, kv_meta, sinks, bsz, nq):
    blk = WINDOW
    n_sub = _largest_tile(nq, SWA_BLOCKS_PER_STEP)
    steps = nq // n_sub
    band, meta = _swa_tables(sinks)
    kv_meta = jnp.pad(kv_meta, ((0, _META_SLOTS - N_META), (0, 0)))
    whole = lambda a: pl.BlockSpec(a.shape, lambda b, i: (0,) * a.ndim)
    return pl.pallas_call(
        _swa_kernel,
        grid=(bsz, steps),
        in_specs=[
            pl.BlockSpec((n_sub * blk, SWA_QD), lambda b, i: (b * steps + i, 0)),
            pl.BlockSpec((n_sub * blk, 2 * SWA_KVD), lambda b, i: (b * steps + i, 0)),
            pl.BlockSpec((blk, 2 * SWA_KVD), lambda b, i: (b * nq + jnp.maximum(i * n_sub - 1, 0), 0)),
            whole(kv_meta), whole(band), whole(meta),
        ],
        out_specs=pl.BlockSpec((n_sub * blk, SWA_QD), lambda b, i: (b * steps + i, 0)),
        out_shape=jax.ShapeDtypeStruct((q.shape[0], SWA_QD), BF16),
        compiler_params=pltpu.CompilerParams(dimension_semantics=("parallel", "parallel"),
                                             vmem_limit_bytes=VMEM_LIMIT),
        name="swa",
    )(q, kv, kv, kv_meta, band, meta)


def _merge_kernel(x_ref, eg_ref, eb_ref, og_ref, os_ref, gates_ref, wg_ref, ws_ref, wo_ref, g1_ref, b1_ref,
                  wr_ref, br_ref, h1_ref, idx_ref, tw_ref):
    d = x_ref.shape[1]
    h0 = _layer_norm(x_ref[...], eg_ref[...], eb_ref[...])
    y_g = _dot(og_ref[...], wg_ref[...])
    y_s = _dot(os_ref[...], ws_ref[...])
    mix = _sigmoid(gates_ref[:, :d].astype(F32)) * y_g + _sigmoid(gates_ref[:, d:].astype(F32)) * y_s
    mixed = _dot(mix.astype(BF16), wo_ref[...])
    h1 = _layer_norm(DEEPNORM_ALPHA * h0 + mixed, g1_ref[...], b1_ref[...])
    _store_token_tiles(h1_ref, h1)

    h_hi, w_hi = h1.astype(BF16), wr_ref[...].astype(BF16)
    h_lo = (h1 - h_hi.astype(F32)).astype(BF16)
    w_lo = (wr_ref[...] - w_hi.astype(F32)).astype(BF16)
    logits = _dot(h_hi, w_hi) + (_dot(h_hi, w_lo) + _dot(h_lo, w_hi)) + br_ref[...]
    lane = lax.broadcasted_iota(jnp.int32, logits.shape, 1)
    vals, idxs = [], []
    for _ in range(TOP_K):
        m = jnp.max(logits, -1, keepdims=True)
        sel = jnp.min(jnp.where(logits == m, lane, LANES), -1, keepdims=True)
        vals.append(m)
        idxs.append(sel)
        logits = jnp.where(lane == sel, NEG_BIG, logits)
    e = [jnp.exp(v - vals[0]) for v in vals]
    tot = e[0] + e[1] + e[2] + e[3]
    for k in range(TOP_K):
        idx_ref[:, k:k + 1] = idxs[k]
        tw_ref[:, k:k + 1] = e[k] / tot


def _merge(x2d, eg, eb, o_gdn, o_swa, gates, wg, ws, wo, g1, b1, wr, br, tm):
    n, d = x2d.shape
    row = lambda i: (i, 0)
    const = lambda i: (0, 0)
    full = lambda a: pl.BlockSpec(a.shape, const)
    return pl.pallas_call(
        _merge_kernel,
        grid=(n // tm,),
        in_specs=[
            pl.BlockSpec((tm, d), row), full(eg), full(eb),
            pl.BlockSpec((tm, GDN_VD), row), pl.BlockSpec((tm, SWA_QD), row), pl.BlockSpec((tm, 2 * d), row),
            full(wg), full(ws), full(wo), full(g1), full(b1), full(wr), full(br),
        ],
        out_specs=[
            pl.BlockSpec((tm * _SUB, LANES), row), pl.BlockSpec((tm, TOP_K), row), pl.BlockSpec((tm, TOP_K), row),
        ],
        out_shape=[
            jax.ShapeDtypeStruct((n * _SUB, LANES), F32),
            jax.ShapeDtypeStruct((n, TOP_K), jnp.int32), jax.ShapeDtypeStruct((n, TOP_K), F32),
        ],
        compiler_params=pltpu.CompilerParams(dimension_semantics=("parallel",), vmem_limit_bytes=VMEM_LIMIT),
        name="merge",
    )(x2d, eg, eb, o_gdn, o_swa, gates, wg, ws, wo, g1, b1, wr, br)


def _rank_kernel(idx_ref, rank_ref, counts_ref, run_ref):
    t = idx_ref.shape[0]

    @pl.when(pl.program_id(0) == 0)
    def _():
        run_ref[...] = jnp.zeros_like(run_ref)

    lane = lax.broadcasted_iota(jnp.int32, (t, LANES), 1)
    hits = [idx_ref[:, k:k + 1] == lane for k in range(TOP_K)]
    onehot = hits[0].astype(F32)
    for k in range(1, TOP_K):
        onehot = onehot + hits[k].astype(F32)
    r = lax.broadcasted_iota(jnp.int32, (t, t), 0)
    c = lax.broadcasted_iota(jnp.int32, (t, t), 1)
    before = (r > c).astype(BF16)
    base = run_ref[...] + _dot(before, onehot.astype(BF16))
    for k in range(TOP_K):
        rank_ref[:, k:k + 1] = jnp.sum(jnp.where(hits[k], base, 0.0), -1, keepdims=True).astype(jnp.int32)
    run_ref[...] = run_ref[...] + jnp.sum(onehot, 0, keepdims=True)
    counts_ref[...] = run_ref[...].astype(jnp.int32)


def _rank(top_idx, t):
    n = top_idx.shape[0]
    return pl.pallas_call(
        _rank_kernel,
        grid=(n // t,),
        in_specs=[pl.BlockSpec((t, TOP_K), lambda i: (i, 0))],
        out_specs=[pl.BlockSpec((t, TOP_K), lambda i: (i, 0)), pl.BlockSpec((1, LANES), lambda i: (0, 0))],
        out_shape=[jax.ShapeDtypeStruct((n, TOP_K), jnp.int32), jax.ShapeDtypeStruct((1, LANES), jnp.int32)],
        scratch_shapes=[pltpu.VMEM((1, LANES), F32)],
        compiler_params=pltpu.CompilerParams(dimension_semantics=("arbitrary",)),
        name="rank",
    )(top_idx)


_ROW_GROUP = 8


def _row_copy(src, dst, sem):
    return pltpu.make_async_copy(src, dst, sem)


def _dispatch_kernel(pad_end_ref, dest_hbm, h_ref, xs_hbm, idx_smem, zeros_ref, sem_idx, sem_rows, sem_zero):
    i = pl.program_id(0)
    t = h_ref.shape[0] // _SUB
    n_idx = t * TOP_K

    @pl.when(i == 0)
    def _():
        zeros_ref[...] = jnp.zeros_like(zeros_ref)
        for e in range(N_EXPERTS):
            end = pad_end_ref[e]
            start = pad_end_ref[e - 1] if e else 0

            @pl.when(end > start)
            def _():
                tile0 = pl.multiple_of(end - MOE_TILE, MOE_TILE)
                fill = _row_copy(zeros_ref, xs_hbm.at[pl.ds(tile0, MOE_TILE)], sem_zero)
                fill.start()
                fill.wait()

    idx_cp = _row_copy(dest_hbm.at[pl.ds(pl.multiple_of(i * n_idx, n_idx), n_idx)], idx_smem, sem_idx)
    idx_cp.start()
    idx_cp.wait()

    def issue(grp, carry):
        row0 = pl.multiple_of(grp * _ROW_GROUP, _ROW_GROUP)
        for r in range(_ROW_GROUP):
            for k in range(TOP_K):
                dst_row = idx_smem[(row0 + r) * TOP_K + k]
                _row_copy(h_ref.at[pl.ds((row0 + r) * _SUB, _SUB)], xs_hbm.at[dst_row],
                          sem_rows).start(priority=k % 2)
        return carry

    lax.fori_loop(0, t // _ROW_GROUP, issue, 0)
    _row_copy(xs_hbm.at[pl.ds(0, n_idx)], xs_hbm.at[pl.ds(0, n_idx)], sem_rows).wait()


def _dispatch(pad_end, dest_flat, h1t, n_rows, t):
    n = h1t.shape[0] // _SUB
    grid_spec = pltpu.PrefetchScalarGridSpec(
        num_scalar_prefetch=1,
        grid=(n // t,),
        in_specs=[
            pl.BlockSpec(memory_space=pl.ANY),
            pl.BlockSpec((t * _SUB, LANES), lambda i, pe: (i, 0)),
        ],
        out_specs=pl.BlockSpec(memory_space=pl.ANY),
        scratch_shapes=[
            pltpu.SMEM((t * TOP_K,), jnp.int32),
            pltpu.VMEM((MOE_TILE, _SUB, LANES), h1t.dtype),
            pltpu.SemaphoreType.DMA, pltpu.SemaphoreType.DMA, pltpu.SemaphoreType.DMA,
        ],
    )
    return pl.pallas_call(
        _dispatch_kernel,
        grid_spec=grid_spec,
        out_shape=jax.ShapeDtypeStruct((n_rows, _SUB, LANES), h1t.dtype),
        compiler_params=pltpu.CompilerParams(dimension_semantics=("arbitrary",), vmem_limit_bytes=VMEM_LIMIT),
        name="dispatch",
    )(pad_end, dest_flat, h1t)


def _expert_kernel(blk_e_ref, n_used_ref, xs_ref, w1_ref, b1_ref, w2_ref, b2_ref, ys_ref):
    del blk_e_ref
    i = pl.program_id(0)
    d_ff = w2_ref.shape[0]

    @pl.when(i < n_used_ref[0])
    def _():
        x = _load_token_tiles(xs_ref, MOE_TILE)
        hdn = _dot(x.astype(BF16), w1_ref[...].astype(BF16)) + b1_ref[...]
        gate = jnp.minimum(hdn[:, :d_ff], SWIGLU_LIMIT)
        up = jnp.clip(hdn[:, d_ff:], -SWIGLU_LIMIT, SWIGLU_LIMIT)
        act = gate * _sigmoid(SWIGLU_ALPHA * gate) * (up + 1.0)
        _store_token_tiles(ys_ref, _dot(act.astype(BF16), w2_ref[...].astype(BF16)) + b2_ref[...])

    @pl.when(i >= n_used_ref[0])
    def _():
        ys_ref[...] = jnp.zeros_like(ys_ref)


def _experts(blk_e, n_used, xs, w1, b1, w2, b2):
    n_e, d, d_ff2 = w1.shape
    d_ff = w2.shape[1]
    n_rows, w = xs.shape[0] // _SUB, LANES
    tile = MOE_TILE * _SUB
    grid_spec = pltpu.PrefetchScalarGridSpec(
        num_scalar_prefetch=2,
        grid=(n_rows // MOE_TILE,),
        in_specs=[
            pl.BlockSpec((tile, w), lambda i, be, nu: (jnp.minimum(i, nu[0] - 1), 0)),
            pl.BlockSpec((None, d, d_ff2), lambda i, be, nu: (be[i], 0, 0)),
            pl.BlockSpec((None, 1, d_ff2), lambda i, be, nu: (be[i], 0, 0)),
            pl.BlockSpec((None, d_ff, d), lambda i, be, nu: (be[i], 0, 0)),
            pl.BlockSpec((None, 1, d), lambda i, be, nu: (be[i], 0, 0)),
        ],
        out_specs=pl.BlockSpec((tile, w), lambda i, be, nu: (i, 0)),
    )
    return pl.pallas_call(
        _expert_kernel,
        grid_spec=grid_spec,
        out_shape=jax.ShapeDtypeStruct((n_rows * _SUB, w), F32),
        compiler_params=pltpu.CompilerParams(dimension_semantics=("arbitrary",), vmem_limit_bytes=VMEM_LIMIT),
        name="experts",
    )(blk_e, n_used, xs, w1, b1.reshape(n_e, 1, d_ff2), w2, b2.reshape(n_e, 1, d))


def _combine_kernel(dest_hbm, ys_hbm, h1_ref, tw_ref, g_ref, b_ref, out_ref, idx_smem, buf, sem_idx, sem_rows):
    i = pl.program_id(0)
    t = out_ref.shape[0]
    n_idx = t * TOP_K

    def start_gather(step, slot):
        idx_cp = _row_copy(dest_hbm.at[pl.ds(pl.multiple_of(step * n_idx, n_idx), n_idx)], idx_smem, sem_idx)
        idx_cp.start()
        idx_cp.wait()

        def issue(grp, carry):
            row0 = pl.multiple_of(grp * _ROW_GROUP, _ROW_GROUP)
            for r in range(_ROW_GROUP):
                for k in range(TOP_K):
                    src_row = idx_smem[(row0 + r) * TOP_K + k]
                    _row_copy(ys_hbm.at[src_row], buf.at[slot, k, pl.ds((row0 + r) * _SUB, _SUB)],
                              sem_rows.at[slot]).start(priority=k % 2)
            return carry

        lax.fori_loop(0, t // _ROW_GROUP, issue, 0)

    @pl.when(i == 0)
    def _():
        start_gather(0, 0)

    @pl.when(i + 1 < pl.num_programs(0))
    def _():
        start_gather(i + 1, (i + 1) % 2)

    slot = i % 2
    _row_copy(buf.at[slot], buf.at[slot], sem_rows.at[slot]).wait()

    pre = DEEPNORM_ALPHA * _load_token_tiles(h1_ref, t)
    for k in range(TOP_K):
        pre = pre + _load_token_tiles(buf, t, lead=(slot, k)) * tw_ref[:, k:k + 1]
    out_ref[...] = _layer_norm(pre, g_ref[...], b_ref[...])


def _combine(dest_flat, ys, h1t, top_w, g2, b2, t):
    n, d = h1t.shape[0] // _SUB, _SUB * LANES
    return pl.pallas_call(
        _combine_kernel,
        grid=(n // t,),
        in_specs=[
            pl.BlockSpec(memory_space=pl.ANY),
            pl.BlockSpec(memory_space=pl.ANY),
            pl.BlockSpec((t * _SUB, LANES), lambda i: (i, 0)),
            pl.BlockSpec((t, TOP_K), lambda i: (i, 0)),
            pl.BlockSpec((1, d), lambda i: (0, 0)),
            pl.BlockSpec((1, d), lambda i: (0, 0)),
        ],
        out_specs=pl.BlockSpec((t, d), lambda i: (i, 0)),
        out_shape=jax.ShapeDtypeStruct((n, d), F32),
        scratch_shapes=[
            pltpu.SMEM((t * TOP_K,), jnp.int32),
            pltpu.VMEM((2, TOP_K, t * _SUB, LANES), ys.dtype),
            pltpu.SemaphoreType.DMA,
            pltpu.SemaphoreType.DMA((2,)),
        ],
        compiler_params=pltpu.CompilerParams(dimension_semantics=("arbitrary",), vmem_limit_bytes=VMEM_LIMIT),
        name="combine",
    )(dest_flat, ys, h1t, top_w, g2, b2)


def _largest_tile(n, cap):
    t = cap
    while n % t:
        t //= 2
    return t


def kernel(x, meta_tokens, emb_ln_g, emb_ln_b, w_in, conv_w, a_log, dt_bias, gdn_norm_w, attn_sinks, w_br_gdn,
           w_br_swa, w_out, ln1_g, ln1_b, w_router, b_router, w_moe1, b_moe1, w_moe2, b_moe2, ln2_g, ln2_b):
    bsz, seq, d = x.shape
    assert seq % WINDOW == 0 and seq % CHUNK == 0 and d == GDN_VD == _SUB * LANES
    n = bsz * seq
    x2d = x.reshape(n, d)
    row1 = lambda v: v.reshape(1, -1)
    l = 0

    sp = np.cumsum((GDN_QK, GDN_QK, GDN_VD, GDN_VD, GDN_HEADS, GDN_HEADS, SWA_QD, SWA_KVD, SWA_KVD, d, d))
    w = w_in[l]
    w_perm = jnp.concatenate([
        w[:, :sp[3]], w[:, sp[5]:sp[10]], w[:, sp[3]:sp[5]],
        jnp.zeros((d, LANES - 2 * GDN_HEADS), w.dtype)], axis=1).astype(BF16)
    eg, eb = row1(emb_ln_g), row1(emb_ln_b)

    tm = _largest_tile(seq, 512)
    conv_w_t = conv_w[l].T
    m_qkv, _, _, m_kv, _, m_ba, m_tail = _in_proj(
        meta_tokens.astype(x.dtype), eg, eb, w_perm, jnp.zeros((_HIST, _C_QKVG), F32), conv_w_t, N_META, 1, True)
    qkv_g, z_g, q_s, kv_s, gates, ba = _in_proj(x2d, eg, eb, w_perm, m_tail, conv_w_t, tm, seq // tm, False)

    n_pad = CHUNK - N_META
    n_chunks = seq // CHUNK
    pad_rows = lambda a: jnp.pad(a, ((n_pad, 0), (0, 0)))
    chunk_t = lambda a, nb: a[:, GDN_HEADS:2 * GDN_HEADS].reshape(nb, CHUNK, GDN_HEADS).transpose(0, 2, 1)
    m_ba_p = pad_rows(m_ba)
    zeros_state = jnp.zeros((GDN_HEADS, GDN_DK, GDN_DV), F32)
    s_meta = _gdn(pad_rows(m_qkv), jnp.zeros((CHUNK, GDN_VD), BF16), m_ba_p, chunk_t(m_ba_p, 1), zeros_state,
                  a_log[l], dt_bias[l], gdn_norm_w[l], 1, 1, n_pad, True)
    o_gdn = _gdn(qkv_g, z_g, ba, chunk_t(ba, bsz * n_chunks), s_meta, a_log[l], dt_bias[l], gdn_norm_w[l],
                 bsz, n_chunks, 0, False)

    o_swa = _swa(q_s, kv_s, m_kv, attn_sinks[l], bsz, seq // WINDOW)

    wr = jnp.pad(w_router[l], ((0, 0), (0, LANES - N_EXPERTS)))
    br = jnp.pad(row1(b_router[l]), ((0, 0), (0, LANES - N_EXPERTS)), constant_values=NEG_BIG)
    h1, top_idx, top_w = _merge(
        x2d, eg, eb, o_gdn, o_swa, gates, w_br_gdn[l].astype(BF16), w_br_swa[l].astype(BF16),
        w_out[l].astype(BF16), row1(ln1_g[l]), row1(ln1_b[l]), wr, br, tm)

    rank, counts = _rank(top_idx, _largest_tile(n, 512))
    counts = counts[0, :N_EXPERTS]
    padded = (counts + MOE_TILE - 1) // MOE_TILE * MOE_TILE
    pad_end = jnp.cumsum(padded)
    pad_start = pad_end - padded
    dest = (pad_start[top_idx] + rank).reshape(-1)
    n_blk = -(-n * TOP_K // MOE_TILE) + N_EXPERTS
    blk_row0 = jnp.arange(n_blk, dtype=jnp.int32) * MOE_TILE
    blk_e = jnp.minimum(jnp.sum((pad_end[None, :] <= blk_row0[:, None]).astype(jnp.int32), axis=1), N_EXPERTS - 1)
    n_used = (pad_end[-1:] // MOE_TILE).astype(jnp.int32)

    tg = _largest_tile(n, GATHER_TOKENS)
    n_rows = n_blk * MOE_TILE
    xs = _dispatch(pad_end.astype(jnp.int32), dest, h1, n_rows, tg)
    ys = _experts(blk_e, n_used, xs.reshape(n_rows * _SUB, LANES), w_moe1[l], b_moe1[l], w_moe2[l], b_moe2[l])
    out = _combine(dest, ys.reshape(n_rows, _SUB, LANES), h1, top_w, row1(ln2_g[l]), row1(ln2_b[l]), tg)
    return out.reshape(bsz, seq, d)
```
